```python
import jax, jax.numpy as jnp
from jax import lax
import numpy as np

D_MODEL = 1024
BATCH = 2
SEQ = 8192
DEPTH = 2

N_MIXERS = 2
BLOCK_Q = 128
EPS = 1e-6
FOX_HEADS = 16
FOX_HEAD_DIM = D_MODEL // FOX_HEADS
FOX_IN = 3 * D_MODEL + FOX_HEADS
FORGET_BIAS_INIT = 3.0
MLA_HEADS = 16
MLA_Q_RANK = 384
MLA_KV_RANK = 256
MLA_NOPE_DIM = 64
MLA_ROPE_DIM = 32
MLA_V_DIM = 64
MLA_QK_DIM = MLA_NOPE_DIM + MLA_ROPE_DIM
MLA_IN = MLA_Q_RANK + MLA_KV_RANK + MLA_ROPE_DIM
ROPE_THETA = 10000.0
N_GROUPS = 4
EXPERTS_PER_GROUP = 4
N_EXPERTS = N_GROUPS * EXPERTS_PER_GROUP
TOP_K_IN_GROUP = 2
D_EXPERT = 256

N_FOX_LAYERS = (DEPTH + 1) // 2
N_MLA_LAYERS = DEPTH // 2

kernel_name = "fox_mla_hier_moe_adaln_trunk"


def rmsnorm(x, g):
    xf = x.astype(jnp.float32)
    y = xf * lax.rsqrt(jnp.mean(xf * xf, axis=-1, keepdims=True) + EPS)
    return (y * g.astype(jnp.float32)).astype(x.dtype)


def rope(x, positions):
    half = MLA_ROPE_DIM // 2
    inv_freq = ROPE_THETA ** (-jnp.arange(0, half, dtype=jnp.float32) / half)
    ang = positions.astype(jnp.float32)[..., None] * inv_freq
    cos = jnp.cos(ang)[:, :, None, :]
    sin = jnp.sin(ang)[:, :, None, :]
    xf = x.astype(jnp.float32)
    x1, x2 = xf[..., :half], xf[..., half:]
    out = jnp.concatenate([x1 * cos - x2 * sin, x1 * sin + x2 * cos], axis=-1)
    return out.astype(x.dtype)


def causal_block_attention(q, k, v, scale, cum=None):
    B, H, S, dk = q.shape
    nb = S // BLOCK_Q
    qb = q.reshape(B, H, nb, BLOCK_Q, dk).transpose(2, 0, 1, 3, 4)
    kpos = jnp.arange(S)

    def one_block(args):
        i, qi = args
        s = jnp.einsum('bhqd,bhkd->bhqk', qi, k).astype(jnp.float32) * scale
        if cum is not None:
            ci = lax.dynamic_slice_in_dim(cum, i * BLOCK_Q, BLOCK_Q, axis=2)
            s = s + ci[..., :, None] - cum[..., None, :]
        qpos = i * BLOCK_Q + jnp.arange(BLOCK_Q)
        mask = kpos[None, :] <= qpos[:, None]
        s = jnp.where(mask, s, -jnp.inf)
        p = jax.nn.softmax(s, axis=-1)
        return jnp.einsum('bhqk,bhkd->bhqd', p.astype(v.dtype), v)

    o = lax.map(one_block, (jnp.arange(nb), qb))
    return o.transpose(1, 2, 0, 3, 4).reshape(B, H, S, v.shape[-1])


def fox_mixer(h, w_in, b_f, q_norm, k_norm, w_out):
    B, S, D = h.shape
    proj = h @ w_in
    q = proj[..., :D].reshape(B, S, FOX_HEADS, FOX_HEAD_DIM)
    k = proj[..., D:2 * D].reshape(B, S, FOX_HEADS, FOX_HEAD_DIM)
    v = proj[..., 2 * D:3 * D].reshape(B, S, FOX_HEADS, FOX_HEAD_DIM)
    f_logit = proj[..., 3 * D:]
    q = rmsnorm(q, q_norm)
    k = rmsnorm(k, k_norm)
    log_f = jax.nn.log_sigmoid((f_logit + b_f).astype(jnp.float32))
    cum = jnp.cumsum(log_f, axis=1).transpose(0, 2, 1)
    o = causal_block_attention(q.transpose(0, 2, 1, 3), k.transpose(0, 2, 1, 3),
                               v.transpose(0, 2, 1, 3), FOX_HEAD_DIM ** -0.5, cum)
    o = o.transpose(0, 2, 1, 3).reshape(B, S, D)
    return o @ w_out


def mla_mixer(h, positions, w_in, q_lat_norm, kv_lat_norm, w_uq, w_ukv, q_norm, k_norm, w_out):
    B, S, D = h.shape
    proj = h @ w_in
    cq = rmsnorm(proj[..., :MLA_Q_RANK], q_lat_norm)
    ckv = rmsnorm(proj[..., MLA_Q_RANK:MLA_Q_RANK + MLA_KV_RANK], kv_lat_norm)
    k_rope = proj[..., MLA_Q_RANK + MLA_KV_RANK:]
    q = (cq @ w_uq).reshape(B, S, MLA_HEADS, MLA_QK_DIM)
    kv = (ckv @ w_ukv).reshape(B, S, MLA_HEADS, MLA_NOPE_DIM + MLA_V_DIM)
    k_nope, v = kv[..., :MLA_NOPE_DIM], kv[..., MLA_NOPE_DIM:]
    k = jnp.concatenate(
        [k_nope, jnp.broadcast_to(k_rope[:, :, None, :], (B, S, MLA_HEADS, MLA_ROPE_DIM))], axis=-1)
    q = rmsnorm(q, q_norm)
    k = rmsnorm(k, k_norm)
    q = jnp.concatenate([q[..., :MLA_NOPE_DIM], rope(q[..., MLA_NOPE_DIM:], positions)], axis=-1)
    k = jnp.concatenate([k[..., :MLA_NOPE_DIM], rope(k[..., MLA_NOPE_DIM:], positions)], axis=-1)
    o = causal_block_attention(q.transpose(0, 2, 1, 3), k.transpose(0, 2, 1, 3),
                               v.transpose(0, 2, 1, 3), MLA_QK_DIM ** -0.5)
    o = o.transpose(0, 2, 1, 3).reshape(B, S, MLA_HEADS * MLA_V_DIM)
    return o @ w_out


def hier_moe(h, w_grp, b_grp, w_rt, b_rt, w_gate, w_up, w_down):
    B, S, D = h.shape
    T = B * S
    t = h.reshape(T, D)
    g_logits = (t @ w_grp).astype(jnp.float32) + b_grp.astype(jnp.float32)
    g_prob = jax.nn.softmax(g_logits, axis=-1)
    g_top, g_idx = lax.top_k(g_prob, 1)
    e_logits = ((t @ w_rt).astype(jnp.float32) + b_rt.astype(jnp.float32)).reshape(
        T, N_GROUPS, EXPERTS_PER_GROUP)
    sel = jnp.broadcast_to(g_idx[:, :, None], (T, 1, EXPERTS_PER_GROUP))
    in_group = jnp.take_along_axis(e_logits, sel, axis=1)[:, 0]
    e_top, e_idx = lax.top_k(in_group, TOP_K_IN_GROUP)
    w = jax.nn.softmax(e_top, axis=-1) * g_top
    expert_id = g_idx * EXPERTS_PER_GROUP + e_idx
    combine = jnp.sum(jax.nn.one_hot(expert_id, N_EXPERTS, dtype=jnp.float32) * w[..., None], axis=1)
    a = jnp.einsum('td,edf->tef', t, w_gate)
    u = jnp.einsum('td,edf->tef', t, w_up)
    hidden = jax.nn.silu(a) * u * combine[..., None].astype(t.dtype)
    out = jnp.einsum('tef,efd->td', hidden, w_down)
    return out.reshape(B, S, D)


def setup_inputs(seed: int = 0) -> dict:
    key = jax.random.key(seed)
    ks = jax.random.split(key, 32)
    D = D_MODEL
    nrm = lambda k, shape, scale: jax.random.normal(k, shape, jnp.float32) * scale
    gain = lambda k, shape: 1.0 + 0.02 * jax.random.normal(k, shape, jnp.float32)
    offsets = jax.random.randint(ks[2], (BATCH, 1), 0, 1024, dtype=jnp.int32)
    return {
        "x": nrm(ks[0], (BATCH, SEQ, D), 1.0),
        "c": nrm(ks[1], (BATCH, D), 1.0),
        "positions": offsets + jnp.arange(SEQ, dtype=jnp.int32)[None, :],
        "w_ada": nrm(ks[3], (DEPTH, D, 6 * D), 0.5 * D ** -0.5),
        "b_ada": nrm(ks[4], (DEPTH, 6 * D), 0.01),
        "mix_norm": gain(ks[5], (DEPTH, D)),
        "ffn_norm": gain(ks[6], (DEPTH, D)),
        "fox_w_in": nrm(ks[7], (N_FOX_LAYERS, D, FOX_IN), D ** -0.5),
        "fox_b_f": FORGET_BIAS_INIT + nrm(ks[8], (N_FOX_LAYERS, FOX_HEADS), 0.5),
        "fox_q_norm": gain(ks[9], (N_FOX_LAYERS, FOX_HEAD_DIM)),
        "fox_k_norm": gain(ks[10], (N_FOX_LAYERS, FOX_HEAD_DIM)),
        "fox_w_out": nrm(ks[11], (N_FOX_LAYERS, D, D), D ** -0.5),
        "mla_w_in": nrm(ks[12], (N_MLA_LAYERS, D, MLA_IN), D ** -0.5),
        "mla_q_lat_norm": gain(ks[13], (N_MLA_LAYERS, MLA_Q_RANK)),
        "mla_kv_lat_norm": gain(ks[14], (N_MLA_LAYERS, MLA_KV_RANK)),
        "mla_w_uq": nrm(ks[15], (N_MLA_LAYERS, MLA_Q_RANK, MLA_HEADS * MLA_QK_DIM), MLA_Q_RANK ** -0.5),
        "mla_w_ukv": nrm(ks[16], (N_MLA_LAYERS, MLA_KV_RANK, MLA_HEADS * (MLA_NOPE_DIM + MLA_V_DIM)),
                         MLA_KV_RANK ** -0.5),
        "mla_q_norm": gain(ks[17], (N_MLA_LAYERS, MLA_QK_DIM)),
        "mla_k_norm": gain(ks[18], (N_MLA_LAYERS, MLA_QK_DIM)),
        "mla_w_out": nrm(ks[19], (N_MLA_LAYERS, MLA_HEADS * MLA_V_DIM, D), (MLA_HEADS * MLA_V_DIM) ** -0.5),
        "moe_w_grp": nrm(ks[20], (DEPTH, D, N_GROUPS), D ** -0.5),
        "moe_b_grp": nrm(ks[21], (DEPTH, N_GROUPS), 0.01),
        "moe_w_rt": nrm(ks[22], (DEPTH, D, N_EXPERTS), D ** -0.5),
        "moe_b_rt": nrm(ks[23], (DEPTH, N_EXPERTS), 0.01),
        "moe_w_gate": nrm(ks[24], (DEPTH, N_EXPERTS, D, D_EXPERT), D ** -0.5),
        "moe_w_up": nrm(ks[25], (DEPTH, N_EXPERTS, D, D_EXPERT), D ** -0.5),
        "moe_w_down": nrm(ks[26], (DEPTH, N_EXPERTS, D_EXPERT, D), D_EXPERT ** -0.5),
    }


def reference(x, c, positions, w_ada, b_ada, mix_norm, ffn_norm,
              fox_w_in, fox_b_f, fox_q_norm, fox_k_norm, fox_w_out,
              mla_w_in, mla_q_lat_norm, mla_kv_lat_norm, mla_w_uq, mla_w_ukv, mla_q_norm, mla_k_norm,
              mla_w_out,
              moe_w_grp, moe_b_grp, moe_w_rt, moe_b_rt, moe_w_gate, moe_w_up, moe_w_down):
    c_act = jax.nn.silu(c)
    for i in range(DEPTH):
        mod = c_act @ w_ada[i] + b_ada[i]
        sh1, sc1, g1, sh2, sc2, g2 = [m[:, None, :] for m in jnp.split(mod, 6, axis=-1)]
        h = rmsnorm(x, mix_norm[i]) * (1.0 + sc1) + sh1
        j = i // N_MIXERS
        if i % N_MIXERS == 0:
            y = fox_mixer(h, fox_w_in[j], fox_b_f[j], fox_q_norm[j], fox_k_norm[j], fox_w_out[j])
        else:
            y = mla_mixer(h, positions, mla_w_in[j], mla_q_lat_norm[j], mla_kv_lat_norm[j],
                          mla_w_uq[j], mla_w_ukv[j], mla_q_norm[j], mla_k_norm[j], mla_w_out[j])
        x = x + g1 * y
        h = rmsnorm(x, ffn_norm[i]) * (1.0 + sc2) + sh2
        x = x + g2 * hier_moe(h, moe_w_grp[i], moe_b_grp[i], moe_w_rt[i], moe_b_rt[i],
                              moe_w_gate[i], moe_w_up[i], moe_w_down[i])
    return x
```

```python
import functools
import math

import numpy as np
import jax
import jax.numpy as jnp
from jax import lax
from jax.experimental import pallas as pl
from jax.experimental.pallas import tpu as pltpu

F32 = jnp.float32
BF16 = jnp.bfloat16

D_MODEL = 1024
HEADS = 16
HEAD_DIM = 64
LANES = 128
PAIRS = HEADS // 2
EPS = 1e-6
MLA_Q_RANK = 384
MLA_KV_RANK = 256
MLA_ROPE = 32
MLA_QK = HEAD_DIM + MLA_ROPE
ROPE_THETA = 10000.0
N_GROUPS = 4
EPG = 4
N_EXPERTS = 16
D_EXPERT = 256

VMEM_LIMIT = 56 * 1024 * 1024

TM_PRE = 512
TM_POST = 512
TM_MOE = 1024
TQ = 512
TK = 512

CUM_HI, CUM_MID, CUM_LO, CUM_ONE = 0, 16, 32, 48


def _params(*sem):
    return pltpu.CompilerParams(dimension_semantics=sem, vmem_limit_bytes=VMEM_LIMIT)


def _sigmoid(x):
    return 1.0 / (1.0 + jnp.exp(-x))


def _modulated_norm(x, gain, scale, shift):
    ms = jnp.mean(x * x, axis=-1, keepdims=True)
    return (x * lax.rsqrt(ms + EPS) * gain) * (1.0 + scale) + shift


def _ada_kernel(c_ref, w_ref, b_ref, o_ref):
    c = c_ref[...]
    act = c * _sigmoid(c)
    o_ref[0] = jnp.dot(act, w_ref[0], preferred_element_type=F32,
                       precision=lax.Precision.HIGHEST) + b_ref[0]


def _ada_modulation(c, w_ada, b_ada):
    depth, d, n = w_ada.shape
    bsz = c.shape[0]
    rows = 8
    tn = 1536
    c_pad = jnp.zeros((rows, d), F32).at[:bsz].set(c)
    out = pl.pallas_call(
        _ada_kernel,
        grid=(depth, n // tn),
        in_specs=[
            pl.BlockSpec((rows, d), lambda i, j: (0, 0)),
            pl.BlockSpec((1, d, tn), lambda i, j: (i, 0, j)),
            pl.BlockSpec((1, 1, tn), lambda i, j: (i, 0, j)),
        ],
        out_specs=pl.BlockSpec((1, rows, tn), lambda i, j: (i, 0, j)),
        out_shape=jax.ShapeDtypeStruct((depth, rows, n), F32),
        compiler_params=_params("arbitrary", "arbitrary"),
        name="ada_mod",
    )(c_pad, w_ada, b_ada.reshape(depth, 1, n))
    return out[:, :bsz]


def _pair_rmsnorm(xp, gain, lo_half):
    sq = xp * xp
    s_all = jnp.sum(sq, axis=-1, keepdims=True)
    s_lo = jnp.sum(jnp.where(lo_half, sq, 0.0), axis=-1, keepdims=True)
    ms = jnp.where(lo_half, s_lo, s_all - s_lo) * (1.0 / HEAD_DIM)
    return xp * lax.rsqrt(ms + EPS) * gain


def _fox_pre_kernel(x_ref, sh_ref, sc_ref, nrm_ref, wqkv_ref, wf_ref, bf_ref, gq_ref, gk_ref,
                    selq_ref, selk_ref, q_ref, k_ref, v_ref, carry_ref):
    tm = x_ref.shape[1]

    @pl.when(pl.program_id(1) == 0)
    def _():
        carry_ref[...] = jnp.zeros_like(carry_ref)

    h = _modulated_norm(x_ref[0], nrm_ref[...], sc_ref[0], sh_ref[0])
    hb = h.astype(BF16)
    proj = jnp.dot(hb, wqkv_ref[...], preferred_element_type=F32)
    fl = jnp.dot(hb, wf_ref[...], preferred_element_type=F32) + bf_ref[...]
    logf = jnp.minimum(fl, 0.0) - jnp.log(1.0 + jnp.exp(-jnp.abs(fl)))

    row = lax.broadcasted_iota(jnp.int32, (tm, tm), 0)
    col = lax.broadcasted_iota(jnp.int32, (tm, tm), 1)
    tri = jnp.where(row >= col, 1.0, 0.0).astype(F32)
    cum = jnp.dot(tri, logf, preferred_element_type=F32,
                  precision=lax.Precision.HIGHEST) + carry_ref[...]
    carry_ref[...] = cum[tm - 1:tm, :]

    lane = lax.broadcasted_iota(jnp.int32, (tm, LANES), 1)
    hi = cum.astype(BF16).astype(F32)
    r1 = cum - hi
    mid = r1.astype(BF16).astype(F32)
    lo = r1 - mid
    parts = jnp.where(lane < CUM_MID, hi,
                      jnp.where(lane < CUM_LO, mid,
                                jnp.where(lane < CUM_ONE, lo,
                                          jnp.where(lane == CUM_ONE, 1.0, 0.0))))
    pb = parts.astype(BF16)
    augq = jnp.dot(pb, selq_ref[...], preferred_element_type=F32)
    augk = jnp.dot(pb, selk_ref[...], preferred_element_type=F32)

    lo_half = lane < HEAD_DIM
    d = D_MODEL
    for p in range(PAIRS):
        qn = _pair_rmsnorm(proj[:, LANES * p:LANES * (p + 1)], gq_ref[...], lo_half) * (HEAD_DIM ** -0.5)
        kn = _pair_rmsnorm(proj[:, d + LANES * p:d + LANES * (p + 1)], gk_ref[...], lo_half)
        e0, e1, e2 = 2 * LANES * p, 2 * LANES * p + LANES, 2 * LANES * (p + 1)
        q_ref[0, :, e0:e1] = jnp.where(lo_half, qn, augq[:, e0:e1]).astype(BF16)
        q_ref[0, :, e1:e2] = jnp.where(lo_half, augq[:, e1:e2], qn).astype(BF16)
        k_ref[0, :, e0:e1] = jnp.where(lo_half, kn, augk[:, e0:e1]).astype(BF16)
        k_ref[0, :, e1:e2] = jnp.where(lo_half, augk[:, e1:e2], kn).astype(BF16)
    v_ref[0] = proj[:, 2 * d:].astype(BF16)


def _fox_selectors():
    selq = np.zeros((LANES, HEADS * LANES), np.float32)
    selk = np.zeros((LANES, HEADS * LANES), np.float32)
    for h in range(HEADS):
        base = h * LANES + (HEAD_DIM if h % 2 == 0 else 0)
        for j, src in enumerate((CUM_HI, CUM_MID, CUM_LO)):
            selq[src + h, base + j] = 1.0
            selq[CUM_ONE, base + 3 + j] = 1.0
            selk[CUM_ONE, base + j] = 1.0
            selk[src + h, base + 3 + j] = -1.0
    return jnp.asarray(selq, BF16), jnp.asarray(selk, BF16)


def _fox_pre(x, sh, sc, nrm, w_in, b_f, q_norm, k_norm):
    bsz, s, d = x.shape
    tm = TM_PRE
    wqkv = w_in[:, :3 * d].astype(BF16)
    wf = w_in[:, 3 * d:]
    zpad = jnp.zeros((d, LANES - 3 * HEADS), F32)
    wf3 = jnp.concatenate([wf, wf, wf, zpad], axis=1).astype(BF16)
    bf3 = jnp.concatenate([b_f, b_f, b_f, jnp.zeros((LANES - 3 * HEADS,), F32)]).reshape(1, LANES)
    gq = jnp.tile(q_norm, 2).reshape(1, LANES)
    gk = jnp.tile(k_norm, 2).reshape(1, LANES)
    selq, selk = _fox_selectors()
    full = lambda shape: pl.BlockSpec(shape, lambda b, t: (0,) * len(shape))
    vec = pl.BlockSpec((1, 1, d), lambda b, t: (b, 0, 0))
    hq = HEADS * LANES
    return pl.pallas_call(
        _fox_pre_kernel,
        grid=(bsz, s // tm),
        in_specs=[
            pl.BlockSpec((1, tm, d), lambda b, t: (b, t, 0)),
            vec, vec, full((1, d)),
            full((d, 3 * d)), full((d, LANES)), full((1, LANES)), full((1, LANES)), full((1, LANES)),
            full((LANES, hq)), full((LANES, hq)),
        ],
        out_specs=[
            pl.BlockSpec((1, tm, hq), lambda b, t: (b, t, 0)),
            pl.BlockSpec((1, tm, hq), lambda b, t: (b, t, 0)),
            pl.BlockSpec((1, tm, d), lambda b, t: (b, t, 0)),
        ],
        out_shape=[
            jax.ShapeDtypeStruct((bsz, s, hq), BF16),
            jax.ShapeDtypeStruct((bsz, s, hq), BF16),
            jax.ShapeDtypeStruct((bsz, s, d), BF16),
        ],
        scratch_shapes=[pltpu.VMEM((1, LANES), F32)],
        compiler_params=_params("arbitrary", "arbitrary"),
        name="fox_pre",
    )(x, sh, sc, nrm.reshape(1, d), wqkv, wf3, bf3, gq, gk, selq, selk)


def _mla_pre_kernel(x_ref, pos_ref, sh_ref, sc_ref, nrm_ref, win_ref, gql_ref, gkvl_ref,
                    wuq_ref, wkn_ref, wv_ref, gq_ref, gk_ref, invf_ref, q_ref, k_ref, v_ref):
    tm = x_ref.shape[1]
    h = _modulated_norm(x_ref[0], nrm_ref[...], sc_ref[0], sh_ref[0])
    proj = jnp.dot(h.astype(BF16), win_ref[...], preferred_element_type=F32)
    cq = proj[:, :MLA_Q_RANK]
    ckv = proj[:, MLA_Q_RANK:MLA_Q_RANK + MLA_KV_RANK]
    kr = proj[:, MLA_Q_RANK + MLA_KV_RANK:]
    cqn = cq * lax.rsqrt(jnp.mean(cq * cq, axis=-1, keepdims=True) + EPS) * gql_ref[...]
    ckvn = ckv * lax.rsqrt(jnp.mean(ckv * ckv, axis=-1, keepdims=True) + EPS) * gkvl_ref[...]
    cqb = cqn.astype(BF16)
    ckvb = ckvn.astype(BF16)
    qall = jnp.dot(cqb, wuq_ref[...], preferred_element_type=F32)
    knall = jnp.dot(ckvb, wkn_ref[...], preferred_element_type=F32)
    v_ref[0] = jnp.dot(ckvb, wv_ref[...], preferred_element_type=F32).astype(BF16)

    lane = lax.broadcasted_iota(jnp.int32, (tm, LANES), 1)
    first_half = lane < HEAD_DIM + MLA_ROPE // 2
    ang = pos_ref[0] * invf_ref[...]
    cosf = jnp.cos(ang)
    sinf = jnp.sin(ang)
    sins = jnp.where(first_half, -sinf, sinf)
    half = MLA_ROPE // 2

    def swap_halves(t):
        return jnp.where(first_half, pltpu.roll(t, LANES - half, axis=1), pltpu.roll(t, half, axis=1))

    gq = gq_ref[...]
    gk = gk_ref[...]
    sw_k = swap_halves(kr * gk) * sins
    inv_dim = 1.0 / MLA_QK
    for hd in range(HEADS):
        sl = slice(LANES * hd, LANES * (hd + 1))
        qh = qall[:, sl]
        rq = lax.rsqrt(jnp.sum(qh * qh, axis=-1, keepdims=True) * inv_dim + EPS)
        qn = qh * rq * gq
        qf = (qn * cosf + swap_halves(qn) * sins) * (MLA_QK ** -0.5)
        q_ref[0, :, sl] = qf.astype(BF16)
        kh = knall[:, sl] + kr
        rk = lax.rsqrt(jnp.sum(kh * kh, axis=-1, keepdims=True) * inv_dim + EPS)
        kf = rk * (kh * gk * cosf + sw_k)
        k_ref[0, :, sl] = kf.astype(BF16)


def _mla_pre(x, positions, sh, sc, nrm, w_in, q_lat_norm, kv_lat_norm, w_uq, w_ukv, q_norm, k_norm):
    bsz, s, d = x.shape
    tm = TM_PRE
    hq = HEADS * LANES
    lat = MLA_Q_RANK + MLA_KV_RANK
    kr_cols = jnp.zeros((d, LANES), F32).at[:, HEAD_DIM:HEAD_DIM + MLA_ROPE].set(w_in[:, lat:])
    win = jnp.concatenate([w_in[:, :lat], kr_cols], axis=1).astype(BF16)
    wuq = jnp.zeros((MLA_Q_RANK, HEADS, LANES), F32).at[:, :, :MLA_QK].set(
        w_uq.reshape(MLA_Q_RANK, HEADS, MLA_QK)).reshape(MLA_Q_RANK, hq).astype(BF16)
    ukv = w_ukv.reshape(MLA_KV_RANK, HEADS, 2 * HEAD_DIM)
    wkn = jnp.zeros((MLA_KV_RANK, HEADS, LANES), F32).at[:, :, :HEAD_DIM].set(
        ukv[:, :, :HEAD_DIM]).reshape(MLA_KV_RANK, hq).astype(BF16)
    wv = ukv[:, :, HEAD_DIM:].reshape(MLA_KV_RANK, HEADS * HEAD_DIM).astype(BF16)
    pad = jnp.zeros((LANES - MLA_QK,), F32)
    gq = jnp.concatenate([q_norm, pad]).reshape(1, LANES)
    gk = jnp.concatenate([k_norm, pad]).reshape(1, LANES)
    half = MLA_ROPE // 2
    inv_freq = ROPE_THETA ** (-jnp.arange(0, half, dtype=F32) / half)
    invf = jnp.zeros((LANES,), F32).at[HEAD_DIM:HEAD_DIM + half].set(inv_freq)
    invf = invf.at[HEAD_DIM + half:HEAD_DIM + MLA_ROPE].set(inv_freq).reshape(1, LANES)
    pos = positions.astype(F32).reshape(bsz, s, 1)
    full = lambda shape: pl.BlockSpec(shape, lambda b, t: (0,) * len(shape))
    vec = pl.BlockSpec((1, 1, d), lambda b, t: (b, 0, 0))
    return pl.pallas_call(
        _mla_pre_kernel,
        grid=(bsz, s // tm),
        in_specs=[
            pl.BlockSpec((1, tm, d), lambda b, t: (b, t, 0)),
            pl.BlockSpec((1, tm, 1), lambda b, t: (b, t, 0)),
            vec, vec, full((1, d)),
            full((d, lat + LANES)), full((1, MLA_Q_RANK)), full((1, MLA_KV_RANK)),
            full((MLA_Q_RANK, hq)), full((MLA_KV_RANK, hq)), full((MLA_KV_RANK, d)),
            full((1, LANES)), full((1, LANES)), full((1, LANES)),
        ],
        out_specs=[
            pl.BlockSpec((1, tm, hq), lambda b, t: (b, t, 0)),
            pl.BlockSpec((1, tm, hq), lambda b, t: (b, t, 0)),
            pl.BlockSpec((1, tm, d), lambda b, t: (b, t, 0)),
        ],
        out_shape=[
            jax.ShapeDtypeStruct((bsz, s, hq), BF16),
            jax.ShapeDtypeStruct((bsz, s, hq), BF16),
            jax.ShapeDtypeStruct((bsz, s, d), BF16),
        ],
        compiler_params=_params("arbitrary", "arbitrary"),
        name="mla_pre",
    )(x, pos, sh, sc, nrm.reshape(1, d), win, q_lat_norm.reshape(1, -1), kv_lat_norm.reshape(1, -1),
      wuq, wkn, wv, gq, gk, invf)


def _attn_kernel(q_ref, k_ref, v_ref, o_ref):
    tq = q_ref.shape[1]
    qi = pl.program_id(2)
    row = lax.broadcasted_iota(jnp.int32, (tq, TK), 0)
    col = lax.broadcasted_iota(jnp.int32, (tq, TK), 1)
    causal = row >= col
    outs = []
    for hh in range(2):
        hs = slice(LANES * hh, LANES * (hh + 1))
        q = q_ref[0, :, hs]

        def step(j, carry, masked):
            m, l, acc = carry
            start = pl.multiple_of(j * TK, TK)
            k = k_ref[0, pl.ds(start, TK), hs]
            v = v_ref[0, pl.ds(start, TK), :]
            s = lax.dot_general(q, k, (((1,), (1,)), ((), ())), preferred_element_type=F32)
            if masked:
                s = jnp.where(causal, s, -jnp.inf)
            m_new = jnp.maximum(m, jnp.max(s, axis=-1, keepdims=True))
            alpha = jnp.exp(m - m_new)
            p = jnp.exp(s - m_new)
            l_new = alpha * l + jnp.sum(p, axis=-1, keepdims=True)
            acc_new = alpha * acc + jnp.dot(p.astype(BF16), v, preferred_element_type=F32)
            return m_new, l_new, acc_new

        init = (jnp.full((tq, 1), -jnp.inf, F32), jnp.zeros((tq, 1), F32), jnp.zeros((tq, LANES), F32))
        carry = lax.fori_loop(0, qi, functools.partial(step, masked=False), init)
        m, l, acc = step(qi, carry, True)
        outs.append(acc / l)
    lane = lax.broadcasted_iota(jnp.int32, (tq, LANES), 1)
    o_ref[0] = jnp.where(lane < HEAD_DIM, outs[0], outs[1]).astype(BF16)


def _attention(q, k, v):
    bsz, s, _ = q.shape
    assert TQ == TK
    return pl.pallas_call(
        _attn_kernel,
        grid=(bsz, PAIRS, s // TQ),
        in_specs=[
            pl.BlockSpec((1, TQ, 2 * LANES), lambda b, p, i: (b, i, p)),
            pl.BlockSpec((1, s, 2 * LANES), lambda b, p, i: (b, 0, p)),
            pl.BlockSpec((1, s, LANES), lambda b, p, i: (b, 0, p)),
        ],
        out_specs=pl.BlockSpec((1, TQ, LANES), lambda b, p, i: (b, i, p)),
        out_shape=jax.ShapeDtypeStruct((bsz, s, PAIRS * LANES), BF16),
        compiler_params=_params("arbitrary", "arbitrary", "arbitrary"),
        name="causal_attn",
    )(q, k, v)


def _post_kernel(o_ref, x_ref, g1_ref, sh_ref, sc_ref, nrm_ref, wout_ref, wrh_ref, wrl_ref, br_ref,
                 x1_ref, h2_ref, comb_ref):
    tm = x_ref.shape[1]
    y = jnp.dot(o_ref[0], wout_ref[...], preferred_element_type=F32)
    x1 = x_ref[0] + g1_ref[0] * y
    x1_ref[0] = x1
    h2 = _modulated_norm(x1, nrm_ref[...], sc_ref[0], sh_ref[0])
    h_hi = h2.astype(BF16)
    h2_ref[0] = h_hi
    h_lo = (h2 - h_hi.astype(F32)).astype(BF16)
    wrh = wrh_ref[...]
    lg = (jnp.dot(h_hi, wrh, preferred_element_type=F32) + jnp.dot(h_lo, wrh, preferred_element_type=F32)
          + jnp.dot(h_hi, wrl_ref[...], preferred_element_type=F32)) + br_ref[...]
    lane = lax.broadcasted_iota(jnp.int32, (tm, LANES), 1)
    lanef = lane.astype(F32)
    neg = -jnp.inf
    is_g = jnp.logical_and(lane >= N_EXPERTS, lane < N_EXPERTS + N_GROUPS)
    gl = jnp.where(is_g, lg, neg)
    gmax = jnp.max(gl, axis=-1, keepdims=True)
    gsum = jnp.sum(jnp.exp(gl - gmax), axis=-1, keepdims=True)
    g_top = 1.0 / gsum
    gidx = jnp.min(jnp.where(gl == gmax, lanef - N_EXPERTS, 1e9), axis=-1, keepdims=True)
    lane_grp = (lane >> 2).astype(F32)
    in_grp = jnp.logical_and(lane < N_EXPERTS, lane_grp == gidx)
    el = jnp.where(in_grp, lg, neg)
    e1 = jnp.max(el, axis=-1, keepdims=True)
    i1 = jnp.min(jnp.where(el == e1, lanef, 1e9), axis=-1, keepdims=True)
    el2 = jnp.where(lanef == i1, neg, el)
    e2 = jnp.max(el2, axis=-1, keepdims=True)
    i2 = jnp.min(jnp.where(el2 == e2, lanef, 1e9), axis=-1, keepdims=True)
    t = jnp.exp(e2 - e1)
    w1 = g_top / (1.0 + t)
    w2 = g_top * t / (1.0 + t)
    comb_ref[0] = jnp.where(lanef == i1, w1, jnp.where(lanef == i2, w2, 0.0))


def _post(o, x, g1, sh, sc, nrm, w_out, w_grp, b_grp, w_rt, b_rt):
    bsz, s, d = x.shape
    tm = TM_POST
    wr = jnp.zeros((d, LANES), F32).at[:, :N_EXPERTS].set(w_rt)
    wr = wr.at[:, N_EXPERTS:N_EXPERTS + N_GROUPS].set(w_grp)
    wrh = wr.astype(BF16)
    wrl = (wr - wrh.astype(F32)).astype(BF16)
    br = jnp.zeros((1, LANES), F32).at[0, :N_EXPERTS].set(b_rt)
    br = br.at[0, N_EXPERTS:N_EXPERTS + N_GROUPS].set(b_grp)
    full = lambda shape: pl.BlockSpec(shape, lambda b, t: (0,) * len(shape))
    vec = pl.BlockSpec((1, 1, d), lambda b, t: (b, 0, 0))
    tile = lambda w: pl.BlockSpec((1, tm, w), lambda b, t: (b, t, 0))
    return pl.pallas_call(
        _post_kernel,
        grid=(bsz, s // tm),
        in_specs=[tile(d), tile(d), vec, vec, vec, full((1, d)), full((d, d)),
                  full((d, LANES)), full((d, LANES)), full((1, LANES))],
        out_specs=[tile(d), tile(d), tile(LANES)],
        out_shape=[
            jax.ShapeDtypeStruct((bsz, s, d), F32),
            jax.ShapeDtypeStruct((bsz, s, d), BF16),
            jax.ShapeDtypeStruct((bsz, s, LANES), F32),
        ],
        compiler_params=_params("arbitrary", "arbitrary"),
        name="post_router",
    )(o, x, g1, sh, sc, nrm.reshape(1, d), w_out.astype(BF16), wrh, wrl, br)


def _moe_kernel(h_ref, comb_ref, x1_ref, g2_ref, wgu_ref, wd_ref, out_ref, acc_ref):
    tm = h_ref.shape[0]
    e = pl.program_id(1)

    @pl.when(e == 0)
    def _():
        acc_ref[...] = jnp.zeros_like(acc_ref)

    au = jnp.dot(h_ref[...], wgu_ref[0], preferred_element_type=F32)
    a = au[:, :D_EXPERT]
    u = au[:, D_EXPERT:]
    lane = lax.broadcasted_iota(jnp.int32, (tm, LANES), 1)
    cw = jnp.sum(jnp.where(lane == e, comb_ref[...], 0.0), axis=-1, keepdims=True)
    hid = (a * _sigmoid(a)) * u * cw
    acc_ref[...] += jnp.dot(hid.astype(BF16), wd_ref[0], preferred_element_type=F32)

    @pl.when(e == N_EXPERTS - 1)
    def _():
        out_ref[...] = x1_ref[...] + g2_ref[0] * acc_ref[...]


def _moe(h2, comb, x1, g2, w_gate, w_up, w_down):
    bsz, s, d = x1.shape
    t = bsz * s
    tm = TM_MOE
    per_batch = s // tm
    wgu = jnp.concatenate([w_gate, w_up], axis=-1).astype(BF16)
    wd = w_down.astype(BF16)
    out = pl.pallas_call(
        _moe_kernel,
        grid=(t // tm, N_EXPERTS),
        in_specs=[
            pl.BlockSpec((tm, d), lambda i, e: (i, 0)),
            pl.BlockSpec((tm, LANES), lambda i, e: (i, 0)),
            pl.BlockSpec((tm, d), lambda i, e: (i, 0)),
            pl.BlockSpec((1, 1, d), lambda i, e: (i // per_batch, 0, 0)),
            pl.BlockSpec((1, d, 2 * D_EXPERT), lambda i, e: (e, 0, 0)),
            pl.BlockSpec((1, D_EXPERT, d), lambda i, e: (e, 0, 0)),
        ],
        out_specs=pl.BlockSpec((tm, d), lambda i, e: (i, 0)),
        out_shape=jax.ShapeDtypeStruct((t, d), F32),
        scratch_shapes=[pltpu.VMEM((tm, d), F32)],
        compiler_params=_params("arbitrary", "arbitrary"),
        name="moe_dense",
    )(h2.reshape(t, d), comb.reshape(t, LANES), x1.reshape(t, d), g2, wgu, wd)
    return out.reshape(bsz, s, d)


def kernel(x, c, positions, w_ada, b_ada, mix_norm, ffn_norm, fox_w_in, fox_b_f, fox_q_norm, fox_k_norm,
           fox_w_out, mla_w_in, mla_q_lat_norm, mla_kv_lat_norm, mla_w_uq, mla_w_ukv, mla_q_norm, mla_k_norm,
           mla_w_out, moe_w_grp, moe_b_grp, moe_w_rt, moe_b_rt, moe_w_gate, moe_w_up, moe_w_down):
    depth = w_ada.shape[0]
    d = x.shape[-1]
    mod = _ada_modulation(c, w_ada, b_ada)
    for i in range(depth):
        sh1, sc1, g1, sh2, sc2, g2 = [mod[i, :, None, d * n:d * (n + 1)] for n in range(6)]
        j = i // 2
        if i % 2 == 0:
            q, k, v = _fox_pre(x, sh1, sc1, mix_norm[i], fox_w_in[j], fox_b_f[j], fox_q_norm[j], fox_k_norm[j])
            w_out = fox_w_out[j]
        else:
            q, k, v = _mla_pre(x, positions, sh1, sc1, mix_norm[i], mla_w_in[j], mla_q_lat_norm[j],
                               mla_kv_lat_norm[j], mla_w_uq[j], mla_w_ukv[j], mla_q_norm[j], mla_k_norm[j])
            w_out = mla_w_out[j]
        o = _attention(q, k, v)
        x1, h2, comb = _post(o, x, g1, sh2, sc2, ffn_norm[i], w_out, moe_w_grp[i], moe_b_grp[i],
                             moe_w_rt[i], moe_b_rt[i])
        x = _moe(h2, comb, x1, g2, moe_w_gate[i], moe_w_up[i], moe_w_down[i])
    return x
```

```python
import functools
import math

import numpy as np
import jax
import jax.numpy as jnp
from jax import lax
from jax.experimental import pallas as pl
from jax.experimental.pallas import tpu as pltpu

F32 = jnp.float32
BF16 = jnp.bfloat16

D_MODEL = 1024
HEADS = 16
HEAD_DIM = 64
LANES = 128
PAIRS = HEADS // 2
EPS = 1e-6
MLA_Q_RANK = 384
MLA_KV_RANK = 256
MLA_ROPE = 32
MLA_QK = HEAD_DIM + MLA_ROPE
ROPE_THETA = 10000.0
N_GROUPS = 4
EPG = 4
N_EXPERTS = 16
D_EXPERT = 256

VMEM_LIMIT = 56 * 1024 * 1024
LOG2E = math.log2(math.e)
MAX_FAST_BOUND = 40.0

TM_PRE = 512
TM_POST = 512
TM_MOE = 1024
TQ = 512
TK = 512

CUM_HI, CUM_MID, CUM_LO, CUM_ONE = 0, 16, 32, 48


def _params(*sem):
    return pltpu.CompilerParams(dimension_semantics=sem, vmem_limit_bytes=VMEM_LIMIT)


def _sigmoid(x):
    return 1.0 / (1.0 + jnp.exp(-x))


def _modulated_norm(x, gain, scale, shift):
    ms = jnp.mean(x * x, axis=-1, keepdims=True)
    return (x * lax.rsqrt(ms + EPS) * gain) * (1.0 + scale) + shift


def _ada_kernel(c_ref, w_ref, b_ref, o_ref):
    c = c_ref[...]
    act = c * _sigmoid(c)
    o_ref[0] = jnp.dot(act, w_ref[0], preferred_element_type=F32,
                       precision=lax.Precision.HIGHEST) + b_ref[0]


def _ada_modulation(c, w_ada, b_ada):
    depth, d, n = w_ada.shape
    bsz = c.shape[0]
    rows = 8
    tn = 1536
    c_pad = jnp.zeros((rows, d), F32).at[:bsz].set(c)
    out = pl.pallas_call(
        _ada_kernel,
        grid=(depth, n // tn),
        in_specs=[
            pl.BlockSpec((rows, d), lambda i, j: (0, 0)),
            pl.BlockSpec((1, d, tn), lambda i, j: (i, 0, j)),
            pl.BlockSpec((1, 1, tn), lambda i, j: (i, 0, j)),
        ],
        out_specs=pl.BlockSpec((1, rows, tn), lambda i, j: (i, 0, j)),
        out_shape=jax.ShapeDtypeStruct((depth, rows, n), F32),
        compiler_params=_params("arbitrary", "arbitrary"),
        name="ada_mod",
    )(c_pad, w_ada, b_ada.reshape(depth, 1, n))
    return out[:, :bsz]


def _pair_rmsnorm(xp, gain, lo_half):
    sq = xp * xp
    s_all = jnp.sum(sq, axis=-1, keepdims=True)
    s_lo = jnp.sum(jnp.where(lo_half, sq, 0.0), axis=-1, keepdims=True)
    ms = jnp.where(lo_half, s_lo, s_all - s_lo) * (1.0 / HEAD_DIM)
    return xp * lax.rsqrt(ms + EPS) * gain


def _value_blocks(vp, lo_half, lane):
    even = jnp.where(lo_half, vp, jnp.where(lane == HEAD_DIM, 1.0, 0.0))
    odd = jnp.where(lo_half, jnp.where(lane == 0, 1.0, 0.0), vp)
    return even.astype(BF16), odd.astype(BF16)


def _fox_pre_kernel(x_ref, sh_ref, sc_ref, nrm_ref, wqkv_ref, wf_ref, bf_ref, gq_ref, gk_ref,
                    selq_ref, selk_ref, off_ref, q_ref, k_ref, v_ref, carry_ref):
    tm = x_ref.shape[1]

    @pl.when(pl.program_id(1) == 0)
    def _():
        carry_ref[...] = jnp.zeros_like(carry_ref)

    h = _modulated_norm(x_ref[0], nrm_ref[...], sc_ref[0], sh_ref[0])
    hb = h.astype(BF16)
    proj = jnp.dot(hb, wqkv_ref[...], preferred_element_type=F32)
    fl = jnp.dot(hb, wf_ref[...], preferred_element_type=F32) + bf_ref[...]
    logf = jnp.minimum(fl, 0.0) - jnp.log(1.0 + jnp.exp(-jnp.abs(fl)))

    row = lax.broadcasted_iota(jnp.int32, (tm, tm), 0)
    col = lax.broadcasted_iota(jnp.int32, (tm, tm), 1)
    tri = jnp.where(row >= col, 1.0, 0.0).astype(F32)
    cum = jnp.dot(tri, logf, preferred_element_type=F32,
                  precision=lax.Precision.HIGHEST) + carry_ref[...]
    carry_ref[...] = cum[tm - 1:tm, :]

    lane = lax.broadcasted_iota(jnp.int32, (tm, LANES), 1)
    cum2 = cum * LOG2E
    hi = cum2.astype(BF16).astype(F32)
    r1 = cum2 - hi
    mid = r1.astype(BF16).astype(F32)
    lo = r1 - mid
    parts = jnp.where(lane < CUM_MID, hi,
                      jnp.where(lane < CUM_LO, mid,
                                jnp.where(lane < CUM_ONE, lo, off_ref[...])))
    pb = parts.astype(BF16)
    augq = jnp.dot(pb, selq_ref[...], preferred_element_type=F32)
    augk = jnp.dot(pb, selk_ref[...], preferred_element_type=F32)

    lo_half = lane < HEAD_DIM
    d = D_MODEL
    for p in range(PAIRS):
        qn = _pair_rmsnorm(proj[:, LANES * p:LANES * (p + 1)], gq_ref[...], lo_half) * (HEAD_DIM ** -0.5 * LOG2E)
        kn = _pair_rmsnorm(proj[:, d + LANES * p:d + LANES * (p + 1)], gk_ref[...], lo_half)
        e0, e1, e2 = 2 * LANES * p, 2 * LANES * p + LANES, 2 * LANES * (p + 1)
        q_ref[0, :, e0:e1] = jnp.where(lo_half, qn, augq[:, e0:e1]).astype(BF16)
        q_ref[0, :, e1:e2] = jnp.where(lo_half, augq[:, e1:e2], qn).astype(BF16)
        k_ref[0, :, e0:e1] = jnp.where(lo_half, kn, augk[:, e0:e1]).astype(BF16)
        k_ref[0, :, e1:e2] = jnp.where(lo_half, augk[:, e1:e2], kn).astype(BF16)
        v_even, v_odd = _value_blocks(proj[:, 2 * d + LANES * p:2 * d + LANES * (p + 1)], lo_half, lane)
        v_ref[0, :, e0:e1] = v_even
        v_ref[0, :, e1:e2] = v_odd


def _fox_selectors():
    selq = np.zeros((LANES, HEADS * LANES), np.float32)
    selk = np.zeros((LANES, HEADS * LANES), np.float32)
    for h in range(HEADS):
        base = h * LANES + (HEAD_DIM if h % 2 == 0 else 0)
        for j, src in enumerate((CUM_HI, CUM_MID, CUM_LO)):
            selq[src + h, base + j] = 1.0
            selq[CUM_ONE, base + 3 + j] = 1.0
            selk[CUM_ONE, base + j] = 1.0
            selk[src + h, base + 3 + j] = -1.0
        for j in range(2):
            selq[CUM_ONE + 1 + j, base + 6 + j] = 1.0
            selk[CUM_ONE, base + 6 + j] = 1.0
    return jnp.asarray(selq, BF16), jnp.asarray(selk, BF16)


def _score_bound(q_gain, k_gain, dim):
    bound = (jnp.max(jnp.abs(q_gain)) * jnp.max(jnp.abs(k_gain)) * (LOG2E * math.sqrt(dim))).astype(F32)
    usable = bound <= MAX_FAST_BOUND
    bound = jnp.where(usable, bound, 0.0)
    hi = bound.astype(BF16).astype(F32)
    return usable, hi, bound - hi


def _fox_pre(x, sh, sc, nrm, w_in, b_f, q_norm, k_norm, bound_hi, bound_lo):
    bsz, s, d = x.shape
    tm = TM_PRE
    wqkv = w_in[:, :3 * d].astype(BF16)
    wf = w_in[:, 3 * d:]
    zpad = jnp.zeros((d, LANES - 3 * HEADS), F32)
    wf3 = jnp.concatenate([wf, wf, wf, zpad], axis=1).astype(BF16)
    bf3 = jnp.concatenate([b_f, b_f, b_f, jnp.zeros((LANES - 3 * HEADS,), F32)]).reshape(1, LANES)
    gq = jnp.tile(q_norm, 2).reshape(1, LANES)
    gk = jnp.tile(k_norm, 2).reshape(1, LANES)
    selq, selk = _fox_selectors()
    off = jnp.zeros((1, LANES), F32).at[0, CUM_ONE].set(1.0)
    off = off.at[0, CUM_ONE + 1].set(-bound_hi).at[0, CUM_ONE + 2].set(-bound_lo)
    full = lambda shape: pl.BlockSpec(shape, lambda b, t: (0,) * len(shape))
    vec = pl.BlockSpec((1, 1, d), lambda b, t: (b, 0, 0))
    hq = HEADS * LANES
    head_blocks = pl.BlockSpec((1, tm, hq), lambda b, t: (b, t, 0))
    return pl.pallas_call(
        _fox_pre_kernel,
        grid=(bsz, s // tm),
        in_specs=[
            pl.BlockSpec((1, tm, d), lambda b, t: (b, t, 0)),
            vec, vec, full((1, d)),
            full((d, 3 * d)), full((d, LANES)), full((1, LANES)), full((1, LANES)), full((1, LANES)),
            full((LANES, hq)), full((LANES, hq)), full((1, LANES)),
        ],
        out_specs=[head_blocks] * 3,
        out_shape=[jax.ShapeDtypeStruct((bsz, s, hq), BF16)] * 3,
        scratch_shapes=[pltpu.VMEM((1, LANES), F32)],
        compiler_params=_params("arbitrary", "arbitrary"),
        name="fox_pre",
    )(x, sh, sc, nrm.reshape(1, d), wqkv, wf3, bf3, gq, gk, selq, selk, off)


def _mla_pre_kernel(x_ref, pos_ref, sh_ref, sc_ref, nrm_ref, win_ref, gql_ref, gkvl_ref,
                    wuq_ref, wkn_ref, wv_ref, gq_ref, gk_ref, invf_ref, qoff_ref, koff_ref,
                    q_ref, k_ref, v_ref):
    tm = x_ref.shape[1]
    h = _modulated_norm(x_ref[0], nrm_ref[...], sc_ref[0], sh_ref[0])
    proj = jnp.dot(h.astype(BF16), win_ref[...], preferred_element_type=F32)
    cq = proj[:, :MLA_Q_RANK]
    ckv = proj[:, MLA_Q_RANK:MLA_Q_RANK + MLA_KV_RANK]
    kr = proj[:, MLA_Q_RANK + MLA_KV_RANK:]
    cqn = cq * lax.rsqrt(jnp.mean(cq * cq, axis=-1, keepdims=True) + EPS) * gql_ref[...]
    ckvn = ckv * lax.rsqrt(jnp.mean(ckv * ckv, axis=-1, keepdims=True) + EPS) * gkvl_ref[...]
    cqb = cqn.astype(BF16)
    ckvb = ckvn.astype(BF16)
    qall = jnp.dot(cqb, wuq_ref[...], preferred_element_type=F32)
    knall = jnp.dot(ckvb, wkn_ref[...], preferred_element_type=F32)
    vall = jnp.dot(ckvb, wv_ref[...], preferred_element_type=F32)

    lane = lax.broadcasted_iota(jnp.int32, (tm, LANES), 1)
    lo_half = lane < HEAD_DIM
    for p in range(PAIRS):
        v_even, v_odd = _value_blocks(vall[:, LANES * p:LANES * (p + 1)], lo_half, lane)
        v_ref[0, :, 2 * LANES * p:2 * LANES * p + LANES] = v_even
        v_ref[0, :, 2 * LANES * p + LANES:2 * LANES * (p + 1)] = v_odd
    first_half = lane < HEAD_DIM + MLA_ROPE // 2
    ang = pos_ref[0] * invf_ref[...]
    cosf = jnp.cos(ang)
    sinf = jnp.sin(ang)
    sins = jnp.where(first_half, -sinf, sinf)
    half = MLA_ROPE // 2

    def swap_halves(t):
        return jnp.where(first_half, pltpu.roll(t, LANES - half, axis=1), pltpu.roll(t, half, axis=1))

    gq = gq_ref[...]
    gk = gk_ref[...]
    sw_k = swap_halves(kr * gk) * sins
    inv_dim = 1.0 / MLA_QK
    for hd in range(HEADS):
        sl = slice(LANES * hd, LANES * (hd + 1))
        qh = qall[:, sl]
        rq = lax.rsqrt(jnp.sum(qh * qh, axis=-1, keepdims=True) * inv_dim + EPS)
        qn = qh * rq * gq
        qf = (qn * cosf + swap_halves(qn) * sins) * (MLA_QK ** -0.5 * LOG2E)
        q_ref[0, :, sl] = (qf + qoff_ref[...]).astype(BF16)
        kh = knall[:, sl] + kr
        rk = lax.rsqrt(jnp.sum(kh * kh, axis=-1, keepdims=True) * inv_dim + EPS)
        kf = rk * (kh * gk * cosf + sw_k)
        k_ref[0, :, sl] = (kf + koff_ref[...]).astype(BF16)


def _mla_pre(x, positions, sh, sc, nrm, w_in, q_lat_norm, kv_lat_norm, w_uq, w_ukv, q_norm, k_norm,
             bound_hi, bound_lo):
    bsz, s, d = x.shape
    tm = TM_PRE
    hq = HEADS * LANES
    lat = MLA_Q_RANK + MLA_KV_RANK
    kr_cols = jnp.zeros((d, LANES), F32).at[:, HEAD_DIM:HEAD_DIM + MLA_ROPE].set(w_in[:, lat:])
    win = jnp.concatenate([w_in[:, :lat], kr_cols], axis=1).astype(BF16)
    wuq = jnp.zeros((MLA_Q_RANK, HEADS, LANES), F32).at[:, :, :MLA_QK].set(
        w_uq.reshape(MLA_Q_RANK, HEADS, MLA_QK)).reshape(MLA_Q_RANK, hq).astype(BF16)
    ukv = w_ukv.reshape(MLA_KV_RANK, HEADS, 2 * HEAD_DIM)
    wkn = jnp.zeros((MLA_KV_RANK, HEADS, LANES), F32).at[:, :, :HEAD_DIM].set(
        ukv[:, :, :HEAD_DIM]).reshape(MLA_KV_RANK, hq).astype(BF16)
    wv = ukv[:, :, HEAD_DIM:].reshape(MLA_KV_RANK, HEADS * HEAD_DIM).astype(BF16)
    pad = jnp.zeros((LANES - MLA_QK,), F32)
    gq = jnp.concatenate([q_norm, pad]).reshape(1, LANES)
    gk = jnp.concatenate([k_norm, pad]).reshape(1, LANES)
    half = MLA_ROPE // 2
    inv_freq = ROPE_THETA ** (-jnp.arange(0, half, dtype=F32) / half)
    invf = jnp.zeros((LANES,), F32).at[HEAD_DIM:HEAD_DIM + half].set(inv_freq)
    invf = invf.at[HEAD_DIM + half:HEAD_DIM + MLA_ROPE].set(inv_freq).reshape(1, LANES)
    pos = positions.astype(F32).reshape(bsz, s, 1)
    qoff = jnp.zeros((1, LANES), F32).at[0, MLA_QK].set(-bound_hi).at[0, MLA_QK + 1].set(-bound_lo)
    koff = jnp.zeros((1, LANES), F32).at[0, MLA_QK:MLA_QK + 2].set(1.0)
    full = lambda shape: pl.BlockSpec(shape, lambda b, t: (0,) * len(shape))
    vec = pl.BlockSpec((1, 1, d), lambda b, t: (b, 0, 0))
    head_blocks = pl.BlockSpec((1, tm, hq), lambda b, t: (b, t, 0))
    return pl.pallas_call(
        _mla_pre_kernel,
        grid=(bsz, s // tm),
        in_specs=[
            pl.BlockSpec((1, tm, d), lambda b, t: (b, t, 0)),
            pl.BlockSpec((1, tm, 1), lambda b, t: (b, t, 0)),
            vec, vec, full((1, d)),
            full((d, lat + LANES)), full((1, MLA_Q_RANK)), full((1, MLA_KV_RANK)),
            full((MLA_Q_RANK, hq)), full((MLA_KV_RANK, hq)), full((MLA_KV_RANK, d)),
            full((1, LANES)), full((1, LANES)), full((1, LANES)), full((1, LANES)), full((1, LANES)),
        ],
        out_specs=[head_blocks] * 3,
        out_shape=[jax.ShapeDtypeStruct((bsz, s, hq), BF16)] * 3,
        compiler_params=_params("arbitrary", "arbitrary"),
        name="mla_pre",
    )(x, pos, sh, sc, nrm.reshape(1, d), win, q_lat_norm.reshape(1, -1), kv_lat_norm.reshape(1, -1),
      wuq, wkn, wv, gq, gk, invf, qoff, koff)


def _attn_tiles(q_ref, k_ref, v_ref, hh, j):
    hs = slice(LANES * hh, LANES * (hh + 1))
    start = pl.multiple_of(j * TK, TK)
    return k_ref[0, pl.ds(start, TK), hs], v_ref[0, pl.ds(start, TK), hs]


def _causal_mask(tq):
    row = lax.broadcasted_iota(jnp.int32, (tq, TK), 0)
    col = lax.broadcasted_iota(jnp.int32, (tq, TK), 1)
    return row >= col


def _attn_online_kernel(q_ref, k_ref, v_ref, o_ref):
    tq = q_ref.shape[1]
    qi = pl.program_id(2)
    causal = _causal_mask(tq)
    qs = [q_ref[0, :, LANES * hh:LANES * (hh + 1)] for hh in range(2)]

    def step(j, carry, masked):
        new = []
        for hh in range(2):
            m, l, acc = carry[hh]
            k, v = _attn_tiles(q_ref, k_ref, v_ref, hh, j)
            s = lax.dot_general(qs[hh], k, (((1,), (1,)), ((), ())), preferred_element_type=F32)
            if masked:
                s = jnp.where(causal, s, -jnp.inf)
            m_new = jnp.maximum(m, jnp.max(s, axis=-1, keepdims=True))
            alpha = jnp.exp2(m - m_new)
            p = jnp.exp2(s - m_new)
            l_new = alpha * l + jnp.sum(p, axis=-1, keepdims=True)
            acc_new = alpha * acc + jnp.dot(p.astype(BF16), v, preferred_element_type=F32)
            new.append((m_new, l_new, acc_new))
        return tuple(new)

    one = (jnp.full((tq, 1), -jnp.inf, F32), jnp.zeros((tq, 1), F32), jnp.zeros((tq, LANES), F32))
    carry = lax.fori_loop(0, qi, functools.partial(step, masked=False), (one, one))
    (_, l0, a0), (_, l1, a1) = step(qi, carry, True)
    lane = lax.broadcasted_iota(jnp.int32, (tq, LANES), 1)
    o_ref[0] = jnp.where(lane < HEAD_DIM, a0 / l0, a1 / l1).astype(BF16)


def _attn_bounded_kernel(q_ref, k_ref, v_ref, o_ref):
    tq = q_ref.shape[1]
    qi = pl.program_id(2)
    causal = _causal_mask(tq)
    qs = [q_ref[0, :, LANES * hh:LANES * (hh + 1)] for hh in range(2)]

    def step(j, accs, masked):
        new = []
        for hh in range(2):
            k, v = _attn_tiles(q_ref, k_ref, v_ref, hh, j)
            s = lax.dot_general(qs[hh], k, (((1,), (1,)), ((), ())), preferred_element_type=F32)
            if masked:
                s = jnp.where(causal, s, -jnp.inf)
            p = jnp.exp2(s).astype(BF16)
            new.append(accs[hh] + jnp.dot(p, v, preferred_element_type=F32))
        return tuple(new)

    def two_steps(jj, accs):
        return step(2 * jj + 1, step(2 * jj, accs, False), False)

    def finish(accs):
        a0, a1 = step(qi, accs, True)
        lane = lax.broadcasted_iota(jnp.int32, (tq, LANES), 1)
        o_ref[0] = jnp.where(lane < HEAD_DIM, a0 / a0[:, HEAD_DIM:HEAD_DIM + 1], a1 / a1[:, 0:1]).astype(BF16)

    zero = jnp.zeros((tq, LANES), F32)
    accs = lax.fori_loop(0, qi >> 1, two_steps, (zero, zero))

    @pl.when((qi & 1) == 1)
    def _():
        finish(step(qi - 1, accs, False))

    @pl.when((qi & 1) == 0)
    def _():
        finish(accs)


def _attention(q, k, v, bounded):
    bsz, s, _ = q.shape
    assert TQ == TK

    def call(body, name):
        return pl.pallas_call(
            body,
            grid=(bsz, PAIRS, s // TQ),
            in_specs=[
                pl.BlockSpec((1, TQ, 2 * LANES), lambda b, p, i: (b, i, p)),
                pl.BlockSpec((1, s, 2 * LANES), lambda b, p, i: (b, 0, p)),
                pl.BlockSpec((1, s, 2 * LANES), lambda b, p, i: (b, 0, p)),
            ],
            out_specs=pl.BlockSpec((1, TQ, LANES), lambda b, p, i: (b, i, p)),
            out_shape=jax.ShapeDtypeStruct((bsz, s, PAIRS * LANES), BF16),
            compiler_params=_params("arbitrary", "arbitrary", "arbitrary"),
            name=name,
        )

    return lax.cond(bounded, call(_attn_bounded_kernel, "attn_bounded"), call(_attn_online_kernel, "attn_online"),
                    q, k, v)


def _post_kernel(o_ref, x_ref, g1_ref, sh_ref, sc_ref, nrm_ref, wout_ref, wrh_ref, wrl_ref, br_ref,
                 x1_ref, h2_ref, comb_ref):
    tm = x_ref.shape[1]
    y = jnp.dot(o_ref[0], wout_ref[...], preferred_element_type=F32)
    x1 = x_ref[0] + g1_ref[0] * y
    x1_ref[0] = x1
    h2 = _modulated_norm(x1, nrm_ref[...], sc_ref[0], sh_ref[0])
    h_hi = h2.astype(BF16)
    h2_ref[0] = h_hi
    h_lo = (h2 - h_hi.astype(F32)).astype(BF16)
    wrh = wrh_ref[...]
    lg = (jnp.dot(h_hi, wrh, preferred_element_type=F32) + jnp.dot(h_lo, wrh, preferred_element_type=F32)
          + jnp.dot(h_hi, wrl_ref[...], preferred_element_type=F32)) + br_ref[...]
    lane = lax.broadcasted_iota(jnp.int32, (tm, LANES), 1)
    lanef = lane.astype(F32)
    neg = -jnp.inf
    is_g = jnp.logical_and(lane >= N_EXPERTS, lane < N_EXPERTS + N_GROUPS)
    gl = jnp.where(is_g, lg, neg)
    gmax = jnp.max(gl, axis=-1, keepdims=True)
    gsum = jnp.sum(jnp.exp(gl - gmax), axis=-1, keepdims=True)
    g_top = 1.0 / gsum
    gidx = jnp.min(jnp.where(gl == gmax, lanef - N_EXPERTS, 1e9), axis=-1, keepdims=True)
    lane_grp = (lane >> 2).astype(F32)
    in_grp = jnp.logical_and(lane < N_EXPERTS, lane_grp == gidx)
    el = jnp.where(in_grp, lg, neg)
    e1 = jnp.max(el, axis=-1, keepdims=True)
    i1 = jnp.min(jnp.where(el == e1, lanef, 1e9), axis=-1, keepdims=True)
    el2 = jnp.where(lanef == i1, neg, el)
    e2 = jnp.max(el2, axis=-1, keepdims=True)
    i2 = jnp.min(jnp.where(el2 == e2, lanef, 1e9), axis=-1, keepdims=True)
    t = jnp.exp(e2 - e1)
    w1 = g_top / (1.0 + t)
    w2 = g_top * t / (1.0 + t)
    comb_ref[0] = jnp.where(lanef == i1, w1, jnp.where(lanef == i2, w2, 0.0))


def _post(o, x, g1, sh, sc, nrm, w_out, w_grp, b_grp, w_rt, b_rt):
    bsz, s, d = x.shape
    tm = TM_POST
    wr = jnp.zeros((d, LANES), F32).at[:, :N_EXPERTS].set(w_rt)
    wr = wr.at[:, N_EXPERTS:N_EXPERTS + N_GROUPS].set(w_grp)
    wrh = wr.astype(BF16)
    wrl = (wr - wrh.astype(F32)).astype(BF16)
    br = jnp.zeros((1, LANES), F32).at[0, :N_EXPERTS].set(b_rt)
    br = br.at[0, N_EXPERTS:N_EXPERTS + N_GROUPS].set(b_grp)
    full = lambda shape: pl.BlockSpec(shape, lambda b, t: (0,) * len(shape))
    vec = pl.BlockSpec((1, 1, d), lambda b, t: (b, 0, 0))
    tile = lambda w: pl.BlockSpec((1, tm, w), lambda b, t: (b, t, 0))
    return pl.pallas_call(
        _post_kernel,
        grid=(bsz, s // tm),
        in_specs=[tile(d), tile(d), vec, vec, vec, full((1, d)), full((d, d)),
                  full((d, LANES)), full((d, LANES)), full((1, LANES))],
        out_specs=[tile(d), tile(d), tile(LANES)],
        out_shape=[
            jax.ShapeDtypeStruct((bsz, s, d), F32),
            jax.ShapeDtypeStruct((bsz, s, d), BF16),
            jax.ShapeDtypeStruct((bsz, s, LANES), F32),
        ],
        compiler_params=_params("arbitrary", "arbitrary"),
        name="post_router",
    )(o, x, g1, sh, sc, nrm.reshape(1, d), w_out.astype(BF16), wrh, wrl, br)


def _moe_kernel(h_ref, comb_ref, x1_ref, g2_ref, wgu_ref, wd_ref, out_ref, acc_ref):
    tm = h_ref.shape[0]
    e = pl.program_id(1)

    @pl.when(e == 0)
    def _():
        acc_ref[...] = jnp.zeros_like(acc_ref)

    au = jnp.dot(h_ref[...], wgu_ref[0], preferred_element_type=F32)
    a = au[:, :D_EXPERT]
    u = au[:, D_EXPERT:]
    lane = lax.broadcasted_iota(jnp.int32, (tm, LANES), 1)
    cw = jnp.sum(jnp.where(lane == e, comb_ref[...], 0.0), axis=-1, keepdims=True)
    hid = (a * _sigmoid(a)) * u * cw
    acc_ref[...] += jnp.dot(hid.astype(BF16), wd_ref[0], preferred_element_type=F32)

    @pl.when(e == N_EXPERTS - 1)
    def _():
        out_ref[...] = x1_ref[...] + g2_ref[0] * acc_ref[...]


def _moe(h2, comb, x1, g2, w_gate, w_up, w_down):
    bsz, s, d = x1.shape
    t = bsz * s
    tm = TM_MOE
    per_batch = s // tm
    wgu = jnp.concatenate([w_gate, w_up], axis=-1).astype(BF16)
    wd = w_down.astype(BF16)
    out = pl.pallas_call(
        _moe_kernel,
        grid=(t // tm, N_EXPERTS),
        in_specs=[
            pl.BlockSpec((tm, d), lambda i, e: (i, 0)),
            pl.BlockSpec((tm, LANES), lambda i, e: (i, 0)),
            pl.BlockSpec((tm, d), lambda i, e: (i, 0)),
            pl.BlockSpec((1, 1, d), lambda i, e: (i // per_batch, 0, 0)),
            pl.BlockSpec((1, d, 2 * D_EXPERT), lambda i, e: (e, 0, 0)),
            pl.BlockSpec((1, D_EXPERT, d), lambda i, e: (e, 0, 0)),
        ],
        out_specs=pl.BlockSpec((tm, d), lambda i, e: (i, 0)),
        out_shape=jax.ShapeDtypeStruct((t, d), F32),
        scratch_shapes=[pltpu.VMEM((tm, d), F32)],
        compiler_params=_params("arbitrary", "arbitrary"),
        name="moe_dense",
    )(h2.reshape(t, d), comb.reshape(t, LANES), x1.reshape(t, d), g2, wgu, wd)
    return out.reshape(bsz, s, d)


def kernel(x, c, positions, w_ada, b_ada, mix_norm, ffn_norm, fox_w_in, fox_b_f, fox_q_norm, fox_k_norm,
           fox_w_out, mla_w_in, mla_q_lat_norm, mla_kv_lat_norm, mla_w_uq, mla_w_ukv, mla_q_norm, mla_k_norm,
           mla_w_out, moe_w_grp, moe_b_grp, moe_w_rt, moe_b_rt, moe_w_gate, moe_w_up, moe_w_down):
    depth = w_ada.shape[0]
    d = x.shape[-1]
    mod = _ada_modulation(c, w_ada, b_ada)
    for i in range(depth):
        sh1, sc1, g1, sh2, sc2, g2 = [mod[i, :, None, d * n:d * (n + 1)] for n in range(6)]
        j = i // 2
        if i % 2 == 0:
            bounded, b_hi, b_lo = _score_bound(fox_q_norm[j], fox_k_norm[j], HEAD_DIM)
            q, k, v = _fox_pre(x, sh1, sc1, mix_norm[i], fox_w_in[j], fox_b_f[j], fox_q_norm[j], fox_k_norm[j],
                               b_hi, b_lo)
            w_out = fox_w_out[j]
        else:
            bounded, b_hi, b_lo = _score_bound(mla_q_norm[j], mla_k_norm[j], MLA_QK)
            q, k, v = _mla_pre(x, positions, sh1, sc1, mix_norm[i], mla_w_in[j], mla_q_lat_norm[j],
                               mla_kv_lat_norm[j], mla_w_uq[j], mla_w_ukv[j], mla_q_norm[j], mla_k_norm[j],
                               b_hi, b_lo)
            w_out = mla_w_out[j]
        o = _attention(q, k, v, bounded)
        x1, h2, comb = _post(o, x, g1, sh2, sc2, ffn_norm[i], w_out, moe_w_grp[i], moe_b_grp[i],
                             moe_w_rt[i], moe_b_rt[i])
        x = _moe(h2, comb, x1, g2, moe_w_gate[i], moe_w_up[i], moe_w_down[i])
    return x
```

```python
import functools
import math

import numpy as np
import jax
import jax.numpy as jnp
from jax import lax
from jax.experimental import pallas as pl
from jax.experimental.pallas import tpu as pltpu

F32 = jnp.float32
BF16 = jnp.bfloat16

D_MODEL = 1024
HEADS = 16
HEAD_DIM = 64
LANES = 128
PAIRS = HEADS // 2
EPS = 1e-6
MLA_Q_RANK = 384
MLA_KV_RANK = 256
MLA_ROPE = 32
MLA_QK = HEAD_DIM + MLA_ROPE
ROPE_THETA = 10000.0
N_GROUPS = 4
EPG = 4
N_EXPERTS = 16
D_EXPERT = 256

VMEM_LIMIT = 56 * 1024 * 1024
LOG2E = math.log2(math.e)
MAX_FAST_BOUND = 40.0

TM_PRE = 512
TM_POST = 512
TM_MOE = 1024
TQ = 512
TK = 512

CUM_HI, CUM_MID, CUM_LO, CUM_ONE = 0, 16, 32, 48


def _params(*sem):
    return pltpu.CompilerParams(dimension_semantics=sem, vmem_limit_bytes=VMEM_LIMIT)


def _sigmoid(x):
    return 1.0 / (1.0 + jnp.exp(-x))


def _modulated_norm(x, gain, scale, shift):
    ms = jnp.mean(x * x, axis=-1, keepdims=True)
    return (x * lax.rsqrt(ms + EPS) * gain) * (1.0 + scale) + shift


def _ada_kernel(c_ref, w_ref, b_ref, o_ref):
    c = c_ref[...]
    act = c * _sigmoid(c)
    o_ref[0] = jnp.dot(act, w_ref[0], preferred_element_type=F32,
                       precision=lax.Precision.HIGHEST) + b_ref[0]


def _ada_modulation(c, w_ada, b_ada):
    depth, d, n = w_ada.shape
    bsz = c.shape[0]
    rows = 8
    tn = 1536
    c_pad = jnp.zeros((rows, d), F32).at[:bsz].set(c)
    out = pl.pallas_call(
        _ada_kernel,
        grid=(depth, n // tn),
        in_specs=[
            pl.BlockSpec((rows, d), lambda i, j: (0, 0)),
            pl.BlockSpec((1, d, tn), lambda i, j: (i, 0, j)),
            pl.BlockSpec((1, 1, tn), lambda i, j: (i, 0, j)),
        ],
        out_specs=pl.BlockSpec((1, rows, tn), lambda i, j: (i, 0, j)),
        out_shape=jax.ShapeDtypeStruct((depth, rows, n), F32),
        compiler_params=_params("arbitrary", "arbitrary"),
        name="ada_mod",
    )(c_pad, w_ada, b_ada.reshape(depth, 1, n))
    return out[:, :bsz]


def _pair_rmsnorm(xp, gain, lo_half):
    sq = xp * xp
    s_all = jnp.sum(sq, axis=-1, keepdims=True)
    s_lo = jnp.sum(jnp.where(lo_half, sq, 0.0), axis=-1, keepdims=True)
    ms = jnp.where(lo_half, s_lo, s_all - s_lo) * (1.0 / HEAD_DIM)
    return xp * lax.rsqrt(ms + EPS) * gain


def _value_blocks(vp, lo_half, lane):
    even = jnp.where(lo_half, vp, jnp.where(lane == HEAD_DIM, 1.0, 0.0))
    odd = jnp.where(lo_half, jnp.where(lane == 0, 1.0, 0.0), vp)
    return even.astype(BF16), odd.astype(BF16)


def _fox_pre_kernel(x_ref, sh_ref, sc_ref, nrm_ref, wqkv_ref, wf_ref, bf_ref, gq_ref, gk_ref,
                    selq_ref, selk_ref, off_ref, q_ref, k_ref, v_ref, carry_ref):
    tm = x_ref.shape[1]

    @pl.when(pl.program_id(1) == 0)
    def _():
        carry_ref[...] = jnp.zeros_like(carry_ref)

    h = _modulated_norm(x_ref[0], nrm_ref[...], sc_ref[0], sh_ref[0])
    hb = h.astype(BF16)
    proj = jnp.dot(hb, wqkv_ref[...], preferred_element_type=F32)
    fl = jnp.dot(hb, wf_ref[...], preferred_element_type=F32) + bf_ref[...]
    logf = jnp.minimum(fl, 0.0) - jnp.log(1.0 + jnp.exp(-jnp.abs(fl)))

    row = lax.broadcasted_iota(jnp.int32, (tm, tm), 0)
    col = lax.broadcasted_iota(jnp.int32, (tm, tm), 1)
    tri = jnp.where(row >= col, 1.0, 0.0).astype(F32)
    cum = jnp.dot(tri, logf, preferred_element_type=F32,
                  precision=lax.Precision.HIGHEST) + carry_ref[...]
    carry_ref[...] = cum[tm - 1:tm, :]

    lane = lax.broadcasted_iota(jnp.int32, (tm, LANES), 1)
    cum2 = cum * LOG2E
    hi = cum2.astype(BF16).astype(F32)
    r1 = cum2 - hi
    mid = r1.astype(BF16).astype(F32)
    lo = r1 - mid
    parts = jnp.where(lane < CUM_MID, hi,
                      jnp.where(lane < CUM_LO, mid,
                                jnp.where(lane < CUM_ONE, lo, off_ref[...])))
    pb = parts.astype(BF16)
    augq = jnp.dot(pb, selq_ref[...], preferred_element_type=F32)
    augk = jnp.dot(pb, selk_ref[...], preferred_element_type=F32)

    lo_half = lane < HEAD_DIM
    d = D_MODEL
    for p in range(PAIRS):
        qn = _pair_rmsnorm(proj[:, LANES * p:LANES * (p + 1)], gq_ref[...], lo_half) * (HEAD_DIM ** -0.5 * LOG2E)
        kn = _pair_rmsnorm(proj[:, d + LANES * p:d + LANES * (p + 1)], gk_ref[...], lo_half)
        e0, e1, e2 = 2 * LANES * p, 2 * LANES * p + LANES, 2 * LANES * (p + 1)
        q_ref[0, :, e0:e1] = jnp.where(lo_half, qn, augq[:, e0:e1]).astype(BF16)
        q_ref[0, :, e1:e2] = jnp.where(lo_half, augq[:, e1:e2], qn).astype(BF16)
        k_ref[0, :, e0:e1] = jnp.where(lo_half, kn, augk[:, e0:e1]).astype(BF16)
        k_ref[0, :, e1:e2] = jnp.where(lo_half, augk[:, e1:e2], kn).astype(BF16)
        v_even, v_odd = _value_blocks(proj[:, 2 * d + LANES * p:2 * d + LANES * (p + 1)], lo_half, lane)
        v_ref[0, :, e0:e1] = v_even
        v_ref[0, :, e1:e2] = v_odd


def _fox_selectors():
    selq = np.zeros((LANES, HEADS * LANES), np.float32)
    selk = np.zeros((LANES, HEADS * LANES), np.float32)
    for h in range(HEADS):
        base = h * LANES + (HEAD_DIM if h % 2 == 0 else 0)
        for j, src in enumerate((CUM_HI, CUM_MID, CUM_LO)):
            selq[src + h, base + j] = 1.0
            selq[CUM_ONE, base + 3 + j] = 1.0
            selk[CUM_ONE, base + j] = 1.0
            selk[src + h, base + 3 + j] = -1.0
        for j in range(2):
            selq[CUM_ONE + 1 + j, base + 6 + j] = 1.0
            selk[CUM_ONE, base + 6 + j] = 1.0
    return jnp.asarray(selq, BF16), jnp.asarray(selk, BF16)


def _score_bound(q_gain, k_gain, dim):
    bound = (jnp.max(jnp.abs(q_gain)) * jnp.max(jnp.abs(k_gain)) * (LOG2E * math.sqrt(dim))).astype(F32)
    usable = bound <= MAX_FAST_BOUND
    bound = jnp.where(usable, bound, 0.0)
    hi = bound.astype(BF16).astype(F32)
    return usable, hi, bound - hi


def _fox_pre(x, sh, sc, nrm, w_in, b_f, q_norm, k_norm, bound_hi, bound_lo):
    bsz, s, d = x.shape
    tm = TM_PRE
    wqkv = w_in[:, :3 * d].astype(BF16)
    wf = w_in[:, 3 * d:]
    zpad = jnp.zeros((d, LANES - 3 * HEADS), F32)
    wf3 = jnp.concatenate([wf, wf, wf, zpad], axis=1).astype(BF16)
    bf3 = jnp.concatenate([b_f, b_f, b_f, jnp.zeros((LANES - 3 * HEADS,), F32)]).reshape(1, LANES)
    gq = jnp.tile(q_norm, 2).reshape(1, LANES)
    gk = jnp.tile(k_norm, 2).reshape(1, LANES)
    selq, selk = _fox_selectors()
    off = jnp.zeros((1, LANES), F32).at[0, CUM_ONE].set(1.0)
    off = off.at[0, CUM_ONE + 1].set(-bound_hi).at[0, CUM_ONE + 2].set(-bound_lo)
    full = lambda shape: pl.BlockSpec(shape, lambda b, t: (0,) * len(shape))
    vec = pl.BlockSpec((1, 1, d), lambda b, t: (b, 0, 0))
    hq = HEADS * LANES
    head_blocks = pl.BlockSpec((1, tm, hq), lambda b, t: (b, t, 0))
    return pl.pallas_call(
        _fox_pre_kernel,
        grid=(bsz, s // tm),
        in_specs=[
            pl.BlockSpec((1, tm, d), lambda b, t: (b, t, 0)),
            vec, vec, full((1, d)),
            full((d, 3 * d)), full((d, LANES)), full((1, LANES)), full((1, LANES)), full((1, LANES)),
            full((LANES, hq)), full((LANES, hq)), full((1, LANES)),
        ],
        out_specs=[head_blocks] * 3,
        out_shape=[jax.ShapeDtypeStruct((bsz, s, hq), BF16)] * 3,
        scratch_shapes=[pltpu.VMEM((1, LANES), F32)],
        compiler_params=_params("arbitrary", "arbitrary"),
        name="fox_pre",
    )(x, sh, sc, nrm.reshape(1, d), wqkv, wf3, bf3, gq, gk, selq, selk, off)


def _mla_pre_kernel(x_ref, pos_ref, sh_ref, sc_ref, nrm_ref, win_ref, gql_ref, gkvl_ref,
                    wuq_ref, wkn_ref, wv_ref, gq_ref, gk_ref, invf_ref, qoff_ref, koff_ref,
                    q_ref, k_ref, v_ref):
    tm = x_ref.shape[1]
    h = _modulated_norm(x_ref[0], nrm_ref[...], sc_ref[0], sh_ref[0])
    proj = jnp.dot(h.astype(BF16), win_ref[...], preferred_element_type=F32)
    cq = proj[:, :MLA_Q_RANK]
    ckv = proj[:, MLA_Q_RANK:MLA_Q_RANK + MLA_KV_RANK]
    kr = proj[:, MLA_Q_RANK + MLA_KV_RANK:]
    cqn = cq * lax.rsqrt(jnp.mean(cq * cq, axis=-1, keepdims=True) + EPS) * gql_ref[...]
    ckvn = ckv * lax.rsqrt(jnp.mean(ckv * ckv, axis=-1, keepdims=True) + EPS) * gkvl_ref[...]
    cqb = cqn.astype(BF16)
    ckvb = ckvn.astype(BF16)
    qall = jnp.dot(cqb, wuq_ref[...], preferred_element_type=F32)
    knall = jnp.dot(ckvb, wkn_ref[...], preferred_element_type=F32)
    vall = jnp.dot(ckvb, wv_ref[...], preferred_element_type=F32)

    lane = lax.broadcasted_iota(jnp.int32, (tm, LANES), 1)
    lo_half = lane < HEAD_DIM
    for p in range(PAIRS):
        v_even, v_odd = _value_blocks(vall[:, LANES * p:LANES * (p + 1)], lo_half, lane)
        v_ref[0, :, 2 * LANES * p:2 * LANES * p + LANES] = v_even
        v_ref[0, :, 2 * LANES * p + LANES:2 * LANES * (p + 1)] = v_odd
    first_half = lane < HEAD_DIM + MLA_ROPE // 2
    ang = pos_ref[0] * invf_ref[...]
    cosf = jnp.cos(ang)
    sinf = jnp.sin(ang)
    sins = jnp.where(first_half, -sinf, sinf)
    half = MLA_ROPE // 2

    def swap_halves(t):
        return jnp.where(first_half, pltpu.roll(t, LANES - half, axis=1), pltpu.roll(t, half, axis=1))

    gq = gq_ref[...]
    gk = gk_ref[...]
    sw_k = swap_halves(kr * gk) * sins
    inv_dim = 1.0 / MLA_QK
    for hd in range(HEADS):
        sl = slice(LANES * hd, LANES * (hd + 1))
        qh = qall[:, sl]
        rq = lax.rsqrt(jnp.sum(qh * qh, axis=-1, keepdims=True) * inv_dim + EPS)
        qn = qh * rq * gq
        qf = (qn * cosf + swap_halves(qn) * sins) * (MLA_QK ** -0.5 * LOG2E)
        q_ref[0, :, sl] = (qf + qoff_ref[...]).astype(BF16)
        kh = knall[:, sl] + kr
        rk = lax.rsqrt(jnp.sum(kh * kh, axis=-1, keepdims=True) * inv_dim + EPS)
        kf = rk * (kh * gk * cosf + sw_k)
        k_ref[0, :, sl] = (kf + koff_ref[...]).astype(BF16)


def _mla_pre(x, positions, sh, sc, nrm, w_in, q_lat_norm, kv_lat_norm, w_uq, w_ukv, q_norm, k_norm,
             bound_hi, bound_lo):
    bsz, s, d = x.shape
    tm = TM_PRE
    hq = HEADS * LANES
    lat = MLA_Q_RANK + MLA_KV_RANK
    kr_cols = jnp.zeros((d, LANES), F32).at[:, HEAD_DIM:HEAD_DIM + MLA_ROPE].set(w_in[:, lat:])
    win = jnp.concatenate([w_in[:, :lat], kr_cols], axis=1).astype(BF16)
    wuq = jnp.zeros((MLA_Q_RANK, HEADS, LANES), F32).at[:, :, :MLA_QK].set(
        w_uq.reshape(MLA_Q_RANK, HEADS, MLA_QK)).reshape(MLA_Q_RANK, hq).astype(BF16)
    ukv = w_ukv.reshape(MLA_KV_RANK, HEADS, 2 * HEAD_DIM)
    wkn = jnp.zeros((MLA_KV_RANK, HEADS, LANES), F32).at[:, :, :HEAD_DIM].set(
        ukv[:, :, :HEAD_DIM]).reshape(MLA_KV_RANK, hq).astype(BF16)
    wv = ukv[:, :, HEAD_DIM:].reshape(MLA_KV_RANK, HEADS * HEAD_DIM).astype(BF16)
    pad = jnp.zeros((LANES - MLA_QK,), F32)
    gq = jnp.concatenate([q_norm, pad]).reshape(1, LANES)
    gk = jnp.concatenate([k_norm, pad]).reshape(1, LANES)
    half = MLA_ROPE // 2
    inv_freq = ROPE_THETA ** (-jnp.arange(0, half, dtype=F32) / half)
    invf = jnp.zeros((LANES,), F32).at[HEAD_DIM:HEAD_DIM + half].set(inv_freq)
    invf = invf.at[HEAD_DIM + half:HEAD_DIM + MLA_ROPE].set(inv_freq).reshape(1, LANES)
    pos = positions.astype(F32).reshape(bsz, s, 1)
    qoff = jnp.zeros((1, LANES), F32).at[0, MLA_QK].set(-bound_hi).at[0, MLA_QK + 1].set(-bound_lo)
    koff = jnp.zeros((1, LANES), F32).at[0, MLA_QK:MLA_QK + 2].set(1.0)
    full = lambda shape: pl.BlockSpec(shape, lambda b, t: (0,) * len(shape))
    vec = pl.BlockSpec((1, 1, d), lambda b, t: (b, 0, 0))
    head_blocks = pl.BlockSpec((1, tm, hq), lambda b, t: (b, t, 0))
    return pl.pallas_call(
        _mla_pre_kernel,
        grid=(bsz, s // tm),
        in_specs=[
            pl.BlockSpec((1, tm, d), lambda b, t: (b, t, 0)),
            pl.BlockSpec((1, tm, 1), lambda b, t: (b, t, 0)),
            vec, vec, full((1, d)),
            full((d, lat + LANES)), full((1, MLA_Q_RANK)), full((1, MLA_KV_RANK)),
            full((MLA_Q_RANK, hq)), full((MLA_KV_RANK, hq)), full((MLA_KV_RANK, d)),
            full((1, LANES)), full((1, LANES)), full((1, LANES)), full((1, LANES)), full((1, LANES)),
        ],
        out_specs=[head_blocks] * 3,
        out_shape=[jax.ShapeDtypeStruct((bsz, s, hq), BF16)] * 3,
        compiler_params=_params("arbitrary", "arbitrary"),
        name="mla_pre",
    )(x, pos, sh, sc, nrm.reshape(1, d), win, q_lat_norm.reshape(1, -1), kv_lat_norm.reshape(1, -1),
      wuq, wkn, wv, gq, gk, invf, qoff, koff)


def _attn_tiles(q_ref, k_ref, v_ref, hh, j):
    hs = slice(LANES * hh, LANES * (hh + 1))
    start = pl.multiple_of(j * TK, TK)
    return k_ref[0, pl.ds(start, TK), hs], v_ref[0, pl.ds(start, TK), hs]


def _causal_mask(tq):
    row = lax.broadcasted_iota(jnp.int32, (tq, TK), 0)
    col = lax.broadcasted_iota(jnp.int32, (tq, TK), 1)
    return row >= col


def _attn_online_kernel(q_ref, k_ref, v_ref, o_ref):
    tq = q_ref.shape[1]
    qi = pl.program_id(2)
    causal = _causal_mask(tq)
    qs = [q_ref[0, :, LANES * hh:LANES * (hh + 1)] for hh in range(2)]

    def step(j, carry, masked):
        new = []
        for hh in range(2):
            m, l, acc = carry[hh]
            k, v = _attn_tiles(q_ref, k_ref, v_ref, hh, j)
            s = lax.dot_general(qs[hh], k, (((1,), (1,)), ((), ())), preferred_element_type=F32)
            if masked:
                s = jnp.where(causal, s, -jnp.inf)
            m_new = jnp.maximum(m, jnp.max(s, axis=-1, keepdims=True))
            alpha = jnp.exp2(m - m_new)
            p = jnp.exp2(s - m_new)
            l_new = alpha * l + jnp.sum(p, axis=-1, keepdims=True)
            acc_new = alpha * acc + jnp.dot(p.astype(BF16), v, preferred_element_type=F32)
            new.append((m_new, l_new, acc_new))
        return tuple(new)

    one = (jnp.full((tq, 1), -jnp.inf, F32), jnp.zeros((tq, 1), F32), jnp.zeros((tq, LANES), F32))
    carry = lax.fori_loop(0, qi, functools.partial(step, masked=False), (one, one))
    (_, l0, a0), (_, l1, a1) = step(qi, carry, True)
    lane = lax.broadcasted_iota(jnp.int32, (tq, LANES), 1)
    o_ref[0] = jnp.where(lane < HEAD_DIM, a0 / l0, a1 / l1).astype(BF16)


def _attn_bounded_kernel(q_ref, k_ref, v_ref, o_ref, *, unroll):
    tq = q_ref.shape[1]
    nh = q_ref.shape[2] // LANES
    qi = pl.program_id(2)
    causal = _causal_mask(tq)
    qs = [q_ref[0, :, LANES * hh:LANES * (hh + 1)] for hh in range(nh)]

    def step(j, accs, masked):
        new = []
        for hh in range(nh):
            k, v = _attn_tiles(q_ref, k_ref, v_ref, hh, j)
            s = lax.dot_general(qs[hh], k, (((1,), (1,)), ((), ())), preferred_element_type=F32)
            if masked:
                s = jnp.where(causal, s, -jnp.inf)
            p = jnp.exp2(s).astype(BF16)
            new.append(accs[hh] + jnp.dot(p, v, preferred_element_type=F32))
        return tuple(new)

    def steps(n):
        def body(jj, accs):
            for u in range(n):
                accs = step(n * jj + u, accs, False)
            return accs
        return body

    def finish(accs):
        accs = step(qi, accs, True)
        lane = lax.broadcasted_iota(jnp.int32, (tq, LANES), 1)
        for p in range(nh // 2):
            a0, a1 = accs[2 * p], accs[2 * p + 1]
            o_ref[0, :, LANES * p:LANES * (p + 1)] = jnp.where(
                lane < HEAD_DIM, a0 / a0[:, HEAD_DIM:HEAD_DIM + 1], a1 / a1[:, 0:1]).astype(BF16)

    accs = (jnp.zeros((tq, LANES), F32),) * nh
    done = 0
    if unroll > 2:
        assert unroll == 4
        accs = lax.fori_loop(0, qi >> 2, steps(4), accs)
        done = (qi >> 2) * 2
    accs = lax.fori_loop(done, qi >> 1, steps(2), accs)

    @pl.when((qi & 1) == 1)
    def _():
        finish(step(qi - 1, accs, False))

    @pl.when((qi & 1) == 0)
    def _():
        finish(accs)


def _attention(q, k, v, bounded, heads_per_step, unroll):
    bsz, s, _ = q.shape
    assert TQ == TK

    def call(body, name, nh):
        w = nh * LANES
        return pl.pallas_call(
            body,
            grid=(bsz, HEADS // nh, s // TQ),
            in_specs=[
                pl.BlockSpec((1, TQ, w), lambda b, g, i: (b, i, g)),
                pl.BlockSpec((1, s, w), lambda b, g, i: (b, 0, g)),
                pl.BlockSpec((1, s, w), lambda b, g, i: (b, 0, g)),
            ],
            out_specs=pl.BlockSpec((1, TQ, w // 2), lambda b, g, i: (b, i, g)),
            out_shape=jax.ShapeDtypeStruct((bsz, s, PAIRS * LANES), BF16),
            compiler_params=_params("arbitrary", "arbitrary", "arbitrary"),
            name=name,
        )

    fast = call(functools.partial(_attn_bounded_kernel, unroll=unroll), "attn_bounded", heads_per_step)
    return lax.cond(bounded, fast, call(_attn_online_kernel, "attn_online", 2), q, k, v)


def _post_kernel(o_ref, x_ref, g1_ref, sh_ref, sc_ref, nrm_ref, wout_ref, wrh_ref, wrl_ref, br_ref,
                 x1_ref, h2_ref, comb_ref):
    tm = x_ref.shape[1]
    y = jnp.dot(o_ref[0], wout_ref[...], preferred_element_type=F32)
    x1 = x_ref[0] + g1_ref[0] * y
    x1_ref[0] = x1
    h2 = _modulated_norm(x1, nrm_ref[...], sc_ref[0], sh_ref[0])
    h_hi = h2.astype(BF16)
    h2_ref[0] = h_hi
    h_lo = (h2 - h_hi.astype(F32)).astype(BF16)
    wrh = wrh_ref[...]
    lg = (jnp.dot(h_hi, wrh, preferred_element_type=F32) + jnp.dot(h_lo, wrh, preferred_element_type=F32)
          + jnp.dot(h_hi, wrl_ref[...], preferred_element_type=F32)) + br_ref[...]
    lane = lax.broadcasted_iota(jnp.int32, (tm, LANES), 1)
    lanef = lane.astype(F32)
    neg = -jnp.inf
    is_g = jnp.logical_and(lane >= N_EXPERTS, lane < N_EXPERTS + N_GROUPS)
    gl = jnp.where(is_g, lg, neg)
    gmax = jnp.max(gl, axis=-1, keepdims=True)
    gsum = jnp.sum(jnp.exp(gl - gmax), axis=-1, keepdims=True)
    g_top = 1.0 / gsum
    gidx = jnp.min(jnp.where(gl == gmax, lanef - N_EXPERTS, 1e9), axis=-1, keepdims=True)
    lane_grp = (lane >> 2).astype(F32)
    in_grp = jnp.logical_and(lane < N_EXPERTS, lane_grp == gidx)
    el = jnp.where(in_grp, lg, neg)
    e1 = jnp.max(el, axis=-1, keepdims=True)
    i1 = jnp.min(jnp.where(el == e1, lanef, 1e9), axis=-1, keepdims=True)
    el2 = jnp.where(lanef == i1, neg, el)
    e2 = jnp.max(el2, axis=-1, keepdims=True)
    i2 = jnp.min(jnp.where(el2 == e2, lanef, 1e9), axis=-1, keepdims=True)
    t = jnp.exp(e2 - e1)
    w1 = g_top / (1.0 + t)
    w2 = g_top * t / (1.0 + t)
    comb_ref[0] = jnp.where(lanef == i1, w1, jnp.where(lanef == i2, w2, 0.0))


def _post(o, x, g1, sh, sc, nrm, w_out, w_grp, b_grp, w_rt, b_rt):
    bsz, s, d = x.shape
    tm = TM_POST
    wr = jnp.zeros((d, LANES), F32).at[:, :N_EXPERTS].set(w_rt)
    wr = wr.at[:, N_EXPERTS:N_EXPERTS + N_GROUPS].set(w_grp)
    wrh = wr.astype(BF16)
    wrl = (wr - wrh.astype(F32)).astype(BF16)
    br = jnp.zeros((1, LANES), F32).at[0, :N_EXPERTS].set(b_rt)
    br = br.at[0, N_EXPERTS:N_EXPERTS + N_GROUPS].set(b_grp)
    full = lambda shape: pl.BlockSpec(shape, lambda b, t: (0,) * len(shape))
    vec = pl.BlockSpec((1, 1, d), lambda b, t: (b, 0, 0))
    tile = lambda w: pl.BlockSpec((1, tm, w), lambda b, t: (b, t, 0))
    return pl.pallas_call(
        _post_kernel,
        grid=(bsz, s // tm),
        in_specs=[tile(d), tile(d), vec, vec, vec, full((1, d)), full((d, d)),
                  full((d, LANES)), full((d, LANES)), full((1, LANES))],
        out_specs=[tile(d), tile(d), tile(LANES)],
        out_shape=[
            jax.ShapeDtypeStruct((bsz, s, d), F32),
            jax.ShapeDtypeStruct((bsz, s, d), BF16),
            jax.ShapeDtypeStruct((bsz, s, LANES), F32),
        ],
        compiler_params=_params("arbitrary", "arbitrary"),
        name="post_router",
    )(o, x, g1, sh, sc, nrm.reshape(1, d), w_out.astype(BF16), wrh, wrl, br)


def _moe_kernel(h_ref, comb_ref, x1_ref, g2_ref, wgu_ref, wd_ref, out_ref, acc_ref):
    tm = h_ref.shape[0]
    e = pl.program_id(1)

    @pl.when(e == 0)
    def _():
        acc_ref[...] = jnp.zeros_like(acc_ref)

    au = jnp.dot(h_ref[...], wgu_ref[0], preferred_element_type=F32)
    a = au[:, :D_EXPERT]
    u = au[:, D_EXPERT:]
    lane = lax.broadcasted_iota(jnp.int32, (tm, LANES), 1)
    cw = jnp.sum(jnp.where(lane == e, comb_ref[...], 0.0), axis=-1, keepdims=True)
    hid = (a * _sigmoid(a)) * u * cw
    acc_ref[...] += jnp.dot(hid.astype(BF16), wd_ref[0], preferred_element_type=F32)

    @pl.when(e == N_EXPERTS - 1)
    def _():
        out_ref[...] = x1_ref[...] + g2_ref[0] * acc_ref[...]


def _moe(h2, comb, x1, g2, w_gate, w_up, w_down):
    bsz, s, d = x1.shape
    t = bsz * s
    tm = TM_MOE
    per_batch = s // tm
    wgu = jnp.concatenate([w_gate, w_up], axis=-1).astype(BF16)
    wd = w_down.astype(BF16)
    out = pl.pallas_call(
        _moe_kernel,
        grid=(t // tm, N_EXPERTS),
        in_specs=[
            pl.BlockSpec((tm, d), lambda i, e: (i, 0)),
            pl.BlockSpec((tm, LANES), lambda i, e: (i, 0)),
            pl.BlockSpec((tm, d), lambda i, e: (i, 0)),
            pl.BlockSpec((1, 1, d), lambda i, e: (i // per_batch, 0, 0)),
            pl.BlockSpec((1, d, 2 * D_EXPERT), lambda i, e: (e, 0, 0)),
            pl.BlockSpec((1, D_EXPERT, d), lambda i, e: (e, 0, 0)),
        ],
        out_specs=pl.BlockSpec((tm, d), lambda i, e: (i, 0)),
        out_shape=jax.ShapeDtypeStruct((t, d), F32),
        scratch_shapes=[pltpu.VMEM((tm, d), F32)],
        compiler_params=_params("arbitrary", "arbitrary"),
        name="moe_dense",
    )(h2.reshape(t, d), comb.reshape(t, LANES), x1.reshape(t, d), g2, wgu, wd)
    return out.reshape(bsz, s, d)


def kernel(x, c, positions, w_ada, b_ada, mix_norm, ffn_norm, fox_w_in, fox_b_f, fox_q_norm, fox_k_norm,
           fox_w_out, mla_w_in, mla_q_lat_norm, mla_kv_lat_norm, mla_w_uq, mla_w_ukv, mla_q_norm, mla_k_norm,
           mla_w_out, moe_w_grp, moe_b_grp, moe_w_rt, moe_b_rt, moe_w_gate, moe_w_up, moe_w_down):
    depth = w_ada.shape[0]
    d = x.shape[-1]
    mod = _ada_modulation(c, w_ada, b_ada)
    for i in range(depth):
        sh1, sc1, g1, sh2, sc2, g2 = [mod[i, :, None, d * n:d * (n + 1)] for n in range(6)]
        j = i // 2
        if i % 2 == 0:
            bounded, b_hi, b_lo = _score_bound(fox_q_norm[j], fox_k_norm[j], HEAD_DIM)
            q, k, v = _fox_pre(x, sh1, sc1, mix_norm[i], fox_w_in[j], fox_b_f[j], fox_q_norm[j], fox_k_norm[j],
                               b_hi, b_lo)
            w_out = fox_w_out[j]
        else:
            bounded, b_hi, b_lo = _score_bound(mla_q_norm[j], mla_k_norm[j], MLA_QK)
            q, k, v = _mla_pre(x, positions, sh1, sc1, mix_norm[i], mla_w_in[j], mla_q_lat_norm[j],
                               mla_kv_lat_norm[j], mla_w_uq[j], mla_w_ukv[j], mla_q_norm[j], mla_k_norm[j],
                               b_hi, b_lo)
            w_out = mla_w_out[j]
        o = _attention(q, k, v, bounded, heads_per_step=4, unroll=2 if i % 2 == 0 else 4)
        x1, h2, comb = _post(o, x, g1, sh2, sc2, ffn_norm[i], w_out, moe_w_grp[i], moe_b_grp[i],
                             moe_w_rt[i], moe_b_rt[i])
        x = _moe(h2, comb, x1, g2, moe_w_gate[i], moe_w_up[i], moe_w_down[i])
    return x
```

```python
import functools
import math

import numpy as np
import jax
import jax.numpy as jnp
from jax import lax
from jax.experimental import pallas as pl
from jax.experimental.pallas import tpu as pltpu

F32 = jnp.float32
BF16 = jnp.bfloat16

D_MODEL = 1024
HEADS = 16
HEAD_DIM = 64
LANES = 128
PAIRS = HEADS // 2
EPS = 1e-6
MLA_Q_RANK = 384
MLA_KV_RANK = 256
MLA_ROPE = 32
MLA_QK = HEAD_DIM + MLA_ROPE
ROPE_THETA = 10000.0
N_GROUPS = 4
EPG = 4
N_EXPERTS = 16
D_EXPERT = 256

VMEM_LIMIT = 56 * 1024 * 1024
LOG2E = math.log2(math.e)
MAX_FAST_BOUND = 40.0
UNDERFLOW_LOG2 = -150.0
FOX_HEADS_PER_STEP = 2
MLA_HEADS_PER_STEP = 4

TM_PRE = 512
TM_POST = 512
TM_MOE = 1024
TQ = 512
TK = 512
assert TM_PRE == TQ == TK

CUM_HI, CUM_MID, CUM_LO, CUM_ONE = 0, 16, 32, 48


def _params(*sem):
    return pltpu.CompilerParams(dimension_semantics=sem, vmem_limit_bytes=VMEM_LIMIT)


def _sigmoid(x):
    return 1.0 / (1.0 + jnp.exp(-x))


def _modulated_norm(x, gain, scale, shift):
    ms = jnp.mean(x * x, axis=-1, keepdims=True)
    return (x * lax.rsqrt(ms + EPS) * gain) * (1.0 + scale) + shift


def _ada_kernel(c_ref, w_ref, b_ref, o_ref):
    c = c_ref[...]
    act = c * _sigmoid(c)
    o_ref[0] = jnp.dot(act, w_ref[0], preferred_element_type=F32,
                       precision=lax.Precision.HIGHEST) + b_ref[0]


def _ada_modulation(c, w_ada, b_ada):
    depth, d, n = w_ada.shape
    bsz = c.shape[0]
    rows = 8
    tn = 1536
    c_pad = jnp.zeros((rows, d), F32).at[:bsz].set(c)
    out = pl.pallas_call(
        _ada_kernel,
        grid=(depth, n // tn),
        in_specs=[
            pl.BlockSpec((rows, d), lambda i, j: (0, 0)),
            pl.BlockSpec((1, d, tn), lambda i, j: (i, 0, j)),
            pl.BlockSpec((1, 1, tn), lambda i, j: (i, 0, j)),
        ],
        out_specs=pl.BlockSpec((1, rows, tn), lambda i, j: (i, 0, j)),
        out_shape=jax.ShapeDtypeStruct((depth, rows, n), F32),
        compiler_params=_params("arbitrary", "arbitrary"),
        name="ada_mod",
    )(c_pad, w_ada, b_ada.reshape(depth, 1, n))
    return out[:, :bsz]


def _pair_rmsnorm(xp, gain, lo_half):
    sq = xp * xp
    s_all = jnp.sum(sq, axis=-1, keepdims=True)
    s_lo = jnp.sum(jnp.where(lo_half, sq, 0.0), axis=-1, keepdims=True)
    ms = jnp.where(lo_half, s_lo, s_all - s_lo) * (1.0 / HEAD_DIM)
    return xp * lax.rsqrt(ms + EPS) * gain


def _value_blocks(vp, lo_half, lane):
    even = jnp.where(lo_half, vp, jnp.where(lane == HEAD_DIM, 1.0, 0.0))
    odd = jnp.where(lo_half, jnp.where(lane == 0, 1.0, 0.0), vp)
    return even.astype(BF16), odd.astype(BF16)


def _fox_pre_kernel(x_ref, sh_ref, sc_ref, nrm_ref, wqkv_ref, wf_ref, bf_ref, gq_ref, gk_ref,
                    selq_ref, selk_ref, off_ref, q_ref, k_ref, v_ref, edge_ref, carry_ref):
    tm = x_ref.shape[1]

    @pl.when(pl.program_id(1) == 0)
    def _():
        carry_ref[...] = jnp.zeros_like(carry_ref)

    h = _modulated_norm(x_ref[0], nrm_ref[...], sc_ref[0], sh_ref[0])
    hb = h.astype(BF16)
    proj = jnp.dot(hb, wqkv_ref[...], preferred_element_type=F32)
    fl = jnp.dot(hb, wf_ref[...], preferred_element_type=F32) + bf_ref[...]
    logf = jnp.minimum(fl, 0.0) - jnp.log(1.0 + jnp.exp(-jnp.abs(fl)))

    row = lax.broadcasted_iota(jnp.int32, (tm, tm), 0)
    col = lax.broadcasted_iota(jnp.int32, (tm, tm), 1)
    tri = jnp.where(row >= col, 1.0, 0.0).astype(F32)
    cum = jnp.dot(tri, logf, preferred_element_type=F32,
                  precision=lax.Precision.HIGHEST) + carry_ref[...]
    carry_ref[...] = cum[tm - 1:tm, :]

    lane = lax.broadcasted_iota(jnp.int32, (tm, LANES), 1)
    cum2 = cum * LOG2E
    edge_ref[0, 0, 0:1, :] = cum2[0:1, :]
    edge_ref[0, 0, 1:2, :] = cum2[tm - 1:tm, :]
    edge_ref[0, 0, 2:8, :] = jnp.zeros((6, LANES), F32)
    hi = cum2.astype(BF16).astype(F32)
    r1 = cum2 - hi
    mid = r1.astype(BF16).astype(F32)
    lo = r1 - mid
    parts = jnp.where(lane < CUM_MID, hi,
                      jnp.where(lane < CUM_LO, mid,
                                jnp.where(lane < CUM_ONE, lo, off_ref[...])))
    pb = parts.astype(BF16)
    augq = jnp.dot(pb, selq_ref[...], preferred_element_type=F32)
    augk = jnp.dot(pb, selk_ref[...], preferred_element_type=F32)

    lo_half = lane < HEAD_DIM
    d = D_MODEL
    for p in range(PAIRS):
        qn = _pair_rmsnorm(proj[:, LANES * p:LANES * (p + 1)], gq_ref[...], lo_half) * (HEAD_DIM ** -0.5 * LOG2E)
        kn = _pair_rmsnorm(proj[:, d + LANES * p:d + LANES * (p + 1)], gk_ref[...], lo_half)
        e0, e1, e2 = 2 * LANES * p, 2 * LANES * p + LANES, 2 * LANES * (p + 1)
        q_ref[0, :, e0:e1] = jnp.where(lo_half, qn, augq[:, e0:e1]).astype(BF16)
        q_ref[0, :, e1:e2] = jnp.where(lo_half, augq[:, e1:e2], qn).astype(BF16)
        k_ref[0, :, e0:e1] = jnp.where(lo_half, kn, augk[:, e0:e1]).astype(BF16)
        k_ref[0, :, e1:e2] = jnp.where(lo_half, augk[:, e1:e2], kn).astype(BF16)
        v_even, v_odd = _value_blocks(proj[:, 2 * d + LANES * p:2 * d + LANES * (p + 1)], lo_half, lane)
        v_ref[0, :, e0:e1] = v_even
        v_ref[0, :, e1:e2] = v_odd


def _fox_selectors():
    selq = np.zeros((LANES, HEADS * LANES), np.float32)
    selk = np.zeros((LANES, HEADS * LANES), np.float32)
    for h in range(HEADS):
        base = h * LANES + (HEAD_DIM if h % 2 == 0 else 0)
        for j, src in enumerate((CUM_HI, CUM_MID, CUM_LO)):
            selq[src + h, base + j] = 1.0
            selq[CUM_ONE, base + 3 + j] = 1.0
            selk[CUM_ONE, base + j] = 1.0
            selk[src + h, base + 3 + j] = -1.0
        for j in range(2):
            selq[CUM_ONE + 1 + j, base + 6 + j] = 1.0
            selk[CUM_ONE, base + 6 + j] = 1.0
    return jnp.asarray(selq, BF16), jnp.asarray(selk, BF16)


def _score_bound(q_gain, k_gain, dim):
    bound = (jnp.max(jnp.abs(q_gain)) * jnp.max(jnp.abs(k_gain)) * (LOG2E * math.sqrt(dim))).astype(F32)
    usable = bound <= MAX_FAST_BOUND
    bound = jnp.where(usable, bound, 0.0)
    hi = bound.astype(BF16).astype(F32)
    return usable, hi, bound - hi


def _fox_pre(x, sh, sc, nrm, w_in, b_f, q_norm, k_norm, bound_hi, bound_lo):
    bsz, s, d = x.shape
    tm = TM_PRE
    wqkv = w_in[:, :3 * d].astype(BF16)
    wf = w_in[:, 3 * d:]
    zpad = jnp.zeros((d, LANES - 3 * HEADS), F32)
    wf3 = jnp.concatenate([wf, wf, wf, zpad], axis=1).astype(BF16)
    bf3 = jnp.concatenate([b_f, b_f, b_f, jnp.zeros((LANES - 3 * HEADS,), F32)]).reshape(1, LANES)
    gq = jnp.tile(q_norm, 2).reshape(1, LANES)
    gk = jnp.tile(k_norm, 2).reshape(1, LANES)
    selq, selk = _fox_selectors()
    off = jnp.zeros((1, LANES), F32).at[0, CUM_ONE].set(1.0)
    off = off.at[0, CUM_ONE + 1].set(-bound_hi).at[0, CUM_ONE + 2].set(-bound_lo)
    full = lambda shape: pl.BlockSpec(shape, lambda b, t: (0,) * len(shape))
    vec = pl.BlockSpec((1, 1, d), lambda b, t: (b, 0, 0))
    hq = HEADS * LANES
    head_blocks = pl.BlockSpec((1, tm, hq), lambda b, t: (b, t, 0))
    return pl.pallas_call(
        _fox_pre_kernel,
        grid=(bsz, s // tm),
        in_specs=[
            pl.BlockSpec((1, tm, d), lambda b, t: (b, t, 0)),
            vec, vec, full((1, d)),
            full((d, 3 * d)), full((d, LANES)), full((1, LANES)), full((1, LANES)), full((1, LANES)),
            full((LANES, hq)), full((LANES, hq)), full((1, LANES)),
        ],
        out_specs=[head_blocks] * 3 + [pl.BlockSpec((1, 1, 8, LANES), lambda b, t: (b, t, 0, 0))],
        out_shape=[jax.ShapeDtypeStruct((bsz, s, hq), BF16)] * 3
        + [jax.ShapeDtypeStruct((bsz, s // tm, 8, LANES), F32)],
        scratch_shapes=[pltpu.VMEM((1, LANES), F32)],
        compiler_params=_params("arbitrary", "arbitrary"),
        name="fox_pre",
    )(x, sh, sc, nrm.reshape(1, d), wqkv, wf3, bf3, gq, gk, selq, selk, off)


def _mla_pre_kernel(x_ref, pos_ref, sh_ref, sc_ref, nrm_ref, win_ref, gql_ref, gkvl_ref,
                    wuq_ref, wuqs_ref, wkn_ref, wv_ref, gq_ref, gqs_ref, gk_ref, invf_ref, qoff_ref, koff_ref,
                    q_ref, k_ref, v_ref):
    tm = x_ref.shape[1]
    h = _modulated_norm(x_ref[0], nrm_ref[...], sc_ref[0], sh_ref[0])
    proj = jnp.dot(h.astype(BF16), win_ref[...], preferred_element_type=F32)
    cq = proj[:, :MLA_Q_RANK]
    ckv = proj[:, MLA_Q_RANK:MLA_Q_RANK + MLA_KV_RANK]
    kr = proj[:, MLA_Q_RANK + MLA_KV_RANK:]
    cqn = cq * lax.rsqrt(jnp.mean(cq * cq, axis=-1, keepdims=True) + EPS) * gql_ref[...]
    ckvn = ckv * lax.rsqrt(jnp.mean(ckv * ckv, axis=-1, keepdims=True) + EPS) * gkvl_ref[...]
    cqb = cqn.astype(BF16)
    ckvb = ckvn.astype(BF16)
    qall = jnp.dot(cqb, wuq_ref[...], preferred_element_type=F32)
    qswap = jnp.dot(cqb, wuqs_ref[...], preferred_element_type=F32)
    knall = jnp.dot(ckvb, wkn_ref[...], preferred_element_type=F32)
    vall = jnp.dot(ckvb, wv_ref[...], preferred_element_type=F32)

    lane = lax.broadcasted_iota(jnp.int32, (tm, LANES), 1)
    lo_half = lane < HEAD_DIM
    for p in range(PAIRS):
        v_even, v_odd = _value_blocks(vall[:, LANES * p:LANES * (p + 1)], lo_half, lane)
        v_ref[0, :, 2 * LANES * p:2 * LANES * p + LANES] = v_even
        v_ref[0, :, 2 * LANES * p + LANES:2 * LANES * (p + 1)] = v_odd
    first_half = lane < HEAD_DIM + MLA_ROPE // 2
    ang = pos_ref[0] * invf_ref[...]
    cosf = jnp.cos(ang)
    sinf = jnp.sin(ang)
    sins = jnp.where(first_half, -sinf, sinf)
    half = MLA_ROPE // 2

    def swap_halves(t):
        return jnp.where(first_half, pltpu.roll(t, LANES - half, axis=1), pltpu.roll(t, half, axis=1))

    gk = gk_ref[...]
    q_cos = gq_ref[...] * cosf
    q_sin = gqs_ref[...] * sins
    k_cos = gk * cosf
    sw_k = swap_halves(kr * gk) * sins
    inv_dim = 1.0 / MLA_QK
    for hd in range(HEADS):
        sl = slice(LANES * hd, LANES * (hd + 1))
        qh = qall[:, sl]
        rq = lax.rsqrt(jnp.sum(qh * qh, axis=-1, keepdims=True) * inv_dim + EPS)
        qf = (qh * q_cos + qswap[:, sl] * q_sin) * rq
        q_ref[0, :, sl] = (qf + qoff_ref[...]).astype(BF16)
        kh = knall[:, sl] + kr
        rk = lax.rsqrt(jnp.sum(kh * kh, axis=-1, keepdims=True) * inv_dim + EPS)
        kf = rk * (kh * k_cos + sw_k)
        k_ref[0, :, sl] = (kf + koff_ref[...]).astype(BF16)


def _mla_pre(x, positions, sh, sc, nrm, w_in, q_lat_norm, kv_lat_norm, w_uq, w_ukv, q_norm, k_norm,
             bound_hi, bound_lo):
    bsz, s, d = x.shape
    tm = TM_PRE
    hq = HEADS * LANES
    lat = MLA_Q_RANK + MLA_KV_RANK
    kr_cols = jnp.zeros((d, LANES), F32).at[:, HEAD_DIM:HEAD_DIM + MLA_ROPE].set(w_in[:, lat:])
    win = jnp.concatenate([w_in[:, :lat], kr_cols], axis=1).astype(BF16)
    half = MLA_ROPE // 2
    r0, r1, r2 = HEAD_DIM, HEAD_DIM + half, HEAD_DIM + MLA_ROPE

    def swap_rotary(t):
        out = jnp.zeros(t.shape[:-1] + (LANES,), F32)
        return out.at[..., r0:r1].set(t[..., r1:r2]).at[..., r1:r2].set(t[..., r0:r1])

    uq = w_uq.reshape(MLA_Q_RANK, HEADS, MLA_QK)
    wuq = jnp.zeros((MLA_Q_RANK, HEADS, LANES), F32).at[:, :, :MLA_QK].set(uq).reshape(
        MLA_Q_RANK, hq).astype(BF16)
    wuqs = swap_rotary(uq).reshape(MLA_Q_RANK, hq).astype(BF16)
    ukv = w_ukv.reshape(MLA_KV_RANK, HEADS, 2 * HEAD_DIM)
    wkn = jnp.zeros((MLA_KV_RANK, HEADS, LANES), F32).at[:, :, :HEAD_DIM].set(
        ukv[:, :, :HEAD_DIM]).reshape(MLA_KV_RANK, hq).astype(BF16)
    wv = ukv[:, :, HEAD_DIM:].reshape(MLA_KV_RANK, HEADS * HEAD_DIM).astype(BF16)
    pad = jnp.zeros((LANES - MLA_QK,), F32)
    gq = jnp.concatenate([q_norm, pad]).reshape(1, LANES) * (MLA_QK ** -0.5 * LOG2E)
    gqs = swap_rotary(gq)
    gk = jnp.concatenate([k_norm, pad]).reshape(1, LANES)
    inv_freq = ROPE_THETA ** (-jnp.arange(0, half, dtype=F32) / half)
    invf = jnp.zeros((LANES,), F32).at[HEAD_DIM:HEAD_DIM + half].set(inv_freq)
    invf = invf.at[HEAD_DIM + half:HEAD_DIM + MLA_ROPE].set(inv_freq).reshape(1, LANES)
    pos = positions.astype(F32).reshape(bsz, s, 1)
    qoff = jnp.zeros((1, LANES), F32).at[0, MLA_QK].set(-bound_hi).at[0, MLA_QK + 1].set(-bound_lo)
    koff = jnp.zeros((1, LANES), F32).at[0, MLA_QK:MLA_QK + 2].set(1.0)
    full = lambda shape: pl.BlockSpec(shape, lambda b, t: (0,) * len(shape))
    vec = pl.BlockSpec((1, 1, d), lambda b, t: (b, 0, 0))
    head_blocks = pl.BlockSpec((1, tm, hq), lambda b, t: (b, t, 0))
    return pl.pallas_call(
        _mla_pre_kernel,
        grid=(bsz, s // tm),
        in_specs=[
            pl.BlockSpec((1, tm, d), lambda b, t: (b, t, 0)),
            pl.BlockSpec((1, tm, 1), lambda b, t: (b, t, 0)),
            vec, vec, full((1, d)),
            full((d, lat + LANES)), full((1, MLA_Q_RANK)), full((1, MLA_KV_RANK)),
            full((MLA_Q_RANK, hq)), full((MLA_Q_RANK, hq)), full((MLA_KV_RANK, hq)), full((MLA_KV_RANK, d)),
            full((1, LANES)), full((1, LANES)), full((1, LANES)), full((1, LANES)), full((1, LANES)),
            full((1, LANES)),
        ],
        out_specs=[head_blocks] * 3,
        out_shape=[jax.ShapeDtypeStruct((bsz, s, hq), BF16)] * 3,
        compiler_params=_params("arbitrary", "arbitrary"),
        name="mla_pre",
    )(x, pos, sh, sc, nrm.reshape(1, d), win, q_lat_norm.reshape(1, -1), kv_lat_norm.reshape(1, -1),
      wuq, wuqs, wkn, wv, gq, gqs, gk, invf, qoff, koff)


def _attn_tiles(q_ref, k_ref, v_ref, hh, j):
    hs = slice(LANES * hh, LANES * (hh + 1))
    start = pl.multiple_of(j * TK, TK)
    return k_ref[0, pl.ds(start, TK), hs], v_ref[0, pl.ds(start, TK), hs]


def _causal_mask(tq):
    row = lax.broadcasted_iota(jnp.int32, (tq, TK), 0)
    col = lax.broadcasted_iota(jnp.int32, (tq, TK), 1)
    return row >= col


def _attn_online_kernel(q_ref, k_ref, v_ref, o_ref):
    tq = q_ref.shape[1]
    qi = pl.program_id(2)
    causal = _causal_mask(tq)
    qs = [q_ref[0, :, LANES * hh:LANES * (hh + 1)] for hh in range(2)]

    def step(j, carry, masked):
        new = []
        for hh in range(2):
            m, l, acc = carry[hh]
            k, v = _attn_tiles(q_ref, k_ref, v_ref, hh, j)
            s = lax.dot_general(qs[hh], k, (((1,), (1,)), ((), ())), preferred_element_type=F32)
            if masked:
                s = jnp.where(causal, s, -jnp.inf)
            m_new = jnp.maximum(m, jnp.max(s, axis=-1, keepdims=True))
            alpha = jnp.exp2(m - m_new)
            p = jnp.exp2(s - m_new)
            l_new = alpha * l + jnp.sum(p, axis=-1, keepdims=True)
            acc_new = alpha * acc + jnp.dot(p.astype(BF16), v, preferred_element_type=F32)
            new.append((m_new, l_new, acc_new))
        return tuple(new)

    one = (jnp.full((tq, 1), -jnp.inf, F32), jnp.zeros((tq, 1), F32), jnp.zeros((tq, LANES), F32))
    carry = lax.fori_loop(0, qi, functools.partial(step, masked=False), (one, one))
    (_, l0, a0), (_, l1, a1) = step(qi, carry, True)
    lane = lax.broadcasted_iota(jnp.int32, (tq, LANES), 1)
    o_ref[0] = jnp.where(lane < HEAD_DIM, a0 / l0, a1 / l1).astype(BF16)


def _first_key_tile(edge_ref, qi, group, nh):
    first = edge_ref[0, qi, 0:1, :]
    skipped = jnp.zeros((1, LANES), F32)
    for j in range(edge_ref.shape[1]):
        skipped += jnp.where(first - edge_ref[0, j, 1:2, :] <= UNDERFLOW_LOG2, 1.0, 0.0)
    lane = lax.broadcasted_iota(jnp.int32, (1, LANES), 1)
    in_group = jnp.logical_and(lane >= group * nh, lane < (group + 1) * nh)
    return jnp.min(jnp.where(in_group, skipped, float(edge_ref.shape[1]))).astype(jnp.int32)


def _attn_bounded_kernel(edge_ref, q_ref, k_ref, v_ref, o_ref):
    tq = q_ref.shape[1]
    nh = q_ref.shape[2] // LANES
    qi = pl.program_id(2)
    j0 = _first_key_tile(edge_ref, qi, pl.program_id(1), nh)
    causal = _causal_mask(tq)
    qs = [q_ref[0, :, LANES * hh:LANES * (hh + 1)] for hh in range(nh)]

    def step(j, accs, masked):
        new = []
        for hh in range(nh):
            k, v = _attn_tiles(q_ref, k_ref, v_ref, hh, j)
            s = lax.dot_general(qs[hh], k, (((1,), (1,)), ((), ())), preferred_element_type=F32)
            if masked:
                s = jnp.where(causal, s, -jnp.inf)
            p = jnp.exp2(s).astype(BF16)
            new.append(accs[hh] + jnp.dot(p, v, preferred_element_type=F32))
        return tuple(new)

    def steps(n, base):
        def body(jj, accs):
            for u in range(n):
                accs = step(base + n * jj + u, accs, False)
            return accs
        return body

    def finish(accs):
        accs = step(qi, accs, True)
        lane = lax.broadcasted_iota(jnp.int32, (tq, LANES), 1)
        for p in range(nh // 2):
            a0, a1 = accs[2 * p], accs[2 * p + 1]
            o_ref[0, :, LANES * p:LANES * (p + 1)] = jnp.where(
                lane < HEAD_DIM, a0 / a0[:, HEAD_DIM:HEAD_DIM + 1], a1 / a1[:, 0:1]).astype(BF16)

    n = qi - j0
    accs = (jnp.zeros((tq, LANES), F32),) * nh
    accs = lax.fori_loop(0, n >> 2, steps(4, j0), accs)
    accs = lax.fori_loop(0, (n >> 1) & 1, steps(2, j0 + ((n >> 2) << 2)), accs)

    @pl.when((n & 1) == 1)
    def _():
        finish(step(qi - 1, accs, False))

    @pl.when((n & 1) == 0)
    def _():
        finish(accs)


def _attention(q, k, v, bounded, edges, heads_per_step):
    bsz, s, _ = q.shape
    assert TQ == TK

    def call(body, name, nh, with_edges):
        w = nh * LANES
        in_specs = [
            pl.BlockSpec((1, TQ, w), lambda b, g, i: (b, i, g)),
            pl.BlockSpec((1, s, w), lambda b, g, i: (b, 0, g)),
            pl.BlockSpec((1, s, w), lambda b, g, i: (b, 0, g)),
        ]
        if with_edges:
            in_specs.insert(0, pl.BlockSpec((1,) + edges.shape[1:], lambda b, g, i: (b, 0, 0, 0)))
        return pl.pallas_call(
            body,
            grid=(bsz, HEADS // nh, s // TQ),
            in_specs=in_specs,
            out_specs=pl.BlockSpec((1, TQ, w // 2), lambda b, g, i: (b, i, g)),
            out_shape=jax.ShapeDtypeStruct((bsz, s, PAIRS * LANES), BF16),
            compiler_params=_params("arbitrary", "arbitrary", "arbitrary"),
            name=name,
        )

    fast = call(_attn_bounded_kernel, "attn_bounded", heads_per_step, True)
    online = call(_attn_online_kernel, "attn_online", 2, False)
    return lax.cond(bounded, fast, lambda e, *qkv: online(*qkv), edges, q, k, v)


def _post_kernel(o_ref, x_ref, g1_ref, sh_ref, sc_ref, nrm_ref, wout_ref, wrh_ref, wrl_ref, br_ref,
                 x1_ref, h2_ref, comb_ref):
    tm = x_ref.shape[1]
    y = jnp.dot(o_ref[0], wout_ref[...], preferred_element_type=F32)
    x1 = x_ref[0] + g1_ref[0] * y
    x1_ref[0] = x1
    h2 = _modulated_norm(x1, nrm_ref[...], sc_ref[0], sh_ref[0])
    h_hi = h2.astype(BF16)
    h2_ref[0] = h_hi
    h_lo = (h2 - h_hi.astype(F32)).astype(BF16)
    wrh = wrh_ref[...]
    lg = (jnp.dot(h_hi, wrh, preferred_element_type=F32) + jnp.dot(h_lo, wrh, preferred_element_type=F32)
          + jnp.dot(h_hi, wrl_ref[...], preferred_element_type=F32)) + br_ref[...]
    lane = lax.broadcasted_iota(jnp.int32, (tm, LANES), 1)
    lanef = lane.astype(F32)
    neg = -jnp.inf
    is_g = jnp.logical_and(lane >= N_EXPERTS, lane < N_EXPERTS + N_GROUPS)
    gl = jnp.where(is_g, lg, neg)
    gmax = jnp.max(gl, axis=-1, keepdims=True)
    gsum = jnp.sum(jnp.exp(gl - gmax), axis=-1, keepdims=True)
    g_top = 1.0 / gsum
    gidx = jnp.min(jnp.where(gl == gmax, lanef - N_EXPERTS, 1e9), axis=-1, keepdims=True)
    lane_grp = (lane >> 2).astype(F32)
    in_grp = jnp.logical_and(lane < N_EXPERTS, lane_grp == gidx)
    el = jnp.where(in_grp, lg, neg)
    e1 = jnp.max(el, axis=-1, keepdims=True)
    i1 = jnp.min(jnp.where(el == e1, lanef, 1e9), axis=-1, keepdims=True)
    el2 = jnp.where(lanef == i1, neg, el)
    e2 = jnp.max(el2, axis=-1, keepdims=True)
    i2 = jnp.min(jnp.where(el2 == e2, lanef, 1e9), axis=-1, keepdims=True)
    t = jnp.exp(e2 - e1)
    w1 = g_top / (1.0 + t)
    w2 = g_top * t / (1.0 + t)
    comb_ref[0] = jnp.where(lanef == i1, w1, jnp.where(lanef == i2, w2, 0.0))


def _post(o, x, g1, sh, sc, nrm, w_out, w_grp, b_grp, w_rt, b_rt):
    bsz, s, d = x.shape
    tm = TM_POST
    wr = jnp.zeros((d, LANES), F32).at[:, :N_EXPERTS].set(w_rt)
    wr = wr.at[:, N_EXPERTS:N_EXPERTS + N_GROUPS].set(w_grp)
    wrh = wr.astype(BF16)
    wrl = (wr - wrh.astype(F32)).astype(BF16)
    br = jnp.zeros((1, LANES), F32).at[0, :N_EXPERTS].set(b_rt)
    br = br.at[0, N_EXPERTS:N_EXPERTS + N_GROUPS].set(b_grp)
    full = lambda shape: pl.BlockSpec(shape, lambda b, t: (0,) * len(shape))
    vec = pl.BlockSpec((1, 1, d), lambda b, t: (b, 0, 0))
    tile = lambda w: pl.BlockSpec((1, tm, w), lambda b, t: (b, t, 0))
    return pl.pallas_call(
        _post_kernel,
        grid=(bsz, s // tm),
        in_specs=[tile(d), tile(d), vec, vec, vec, full((1, d)), full((d, d)),
                  full((d, LANES)), full((d, LANES)), full((1, LANES))],
        out_specs=[tile(d), tile(d), tile(LANES)],
        out_shape=[
            jax.ShapeDtypeStruct((bsz, s, d), F32),
            jax.ShapeDtypeStruct((bsz, s, d), BF16),
            jax.ShapeDtypeStruct((bsz, s, LANES), F32),
        ],
        compiler_params=_params("arbitrary", "arbitrary"),
        name="post_router",
    )(o, x, g1, sh, sc, nrm.reshape(1, d), w_out.astype(BF16), wrh, wrl, br)


def _moe_kernel(h_ref, comb_ref, x1_ref, g2_ref, wgu_ref, wd_ref, out_ref, acc_ref):
    tm = h_ref.shape[0]
    e = pl.program_id(1)

    @pl.when(e == 0)
    def _():
        acc_ref[...] = jnp.zeros_like(acc_ref)

    au = jnp.dot(h_ref[...], wgu_ref[0], preferred_element_type=F32)
    a = au[:, :D_EXPERT]
    u = au[:, D_EXPERT:]
    lane = lax.broadcasted_iota(jnp.int32, (tm, LANES), 1)
    cw = jnp.sum(jnp.where(lane == e, comb_ref[...], 0.0), axis=-1, keepdims=True)
    hid = (a * _sigmoid(a)) * u * cw
    acc_ref[...] += jnp.dot(hid.astype(BF16), wd_ref[0], preferred_element_type=F32)

    @pl.when(e == N_EXPERTS - 1)
    def _():
        out_ref[...] = x1_ref[...] + g2_ref[0] * acc_ref[...]


def _moe(h2, comb, x1, g2, w_gate, w_up, w_down):
    bsz, s, d = x1.shape
    t = bsz * s
    tm = TM_MOE
    per_batch = s // tm
    wgu = jnp.concatenate([w_gate, w_up], axis=-1).astype(BF16)
    wd = w_down.astype(BF16)
    out = pl.pallas_call(
        _moe_kernel,
        grid=(t // tm, N_EXPERTS),
        in_specs=[
            pl.BlockSpec((tm, d), lambda i, e: (i, 0)),
            pl.BlockSpec((tm, LANES), lambda i, e: (i, 0)),
            pl.BlockSpec((tm, d), lambda i, e: (i, 0)),
            pl.BlockSpec((1, 1, d), lambda i, e: (i // per_batch, 0, 0)),
            pl.BlockSpec((1, d, 2 * D_EXPERT), lambda i, e: (e, 0, 0)),
            pl.BlockSpec((1, D_EXPERT, d), lambda i, e: (e, 0, 0)),
        ],
        out_specs=pl.BlockSpec((tm, d), lambda i, e: (i, 0)),
        out_shape=jax.ShapeDtypeStruct((t, d), F32),
        scratch_shapes=[pltpu.VMEM((tm, d), F32)],
        compiler_params=_params("arbitrary", "arbitrary"),
        name="moe_dense",
    )(h2.reshape(t, d), comb.reshape(t, LANES), x1.reshape(t, d), g2, wgu, wd)
    return out.reshape(bsz, s, d)


def kernel(x, c, positions, w_ada, b_ada, mix_norm, ffn_norm, fox_w_in, fox_b_f, fox_q_norm, fox_k_norm,
           fox_w_out, mla_w_in, mla_q_lat_norm, mla_kv_lat_norm, mla_w_uq, mla_w_ukv, mla_q_norm, mla_k_norm,
           mla_w_out, moe_w_grp, moe_b_grp, moe_w_rt, moe_b_rt, moe_w_gate, moe_w_up, moe_w_down):
    depth = w_ada.shape[0]
    d = x.shape[-1]
    mod = _ada_modulation(c, w_ada, b_ada)
    for i in range(depth):
        sh1, sc1, g1, sh2, sc2, g2 = [mod[i, :, None, d * n:d * (n + 1)] for n in range(6)]
        j = i // 2
        if i % 2 == 0:
            bounded, b_hi, b_lo = _score_bound(fox_q_norm[j], fox_k_norm[j], HEAD_DIM)
            q, k, v, edges = _fox_pre(x, sh1, sc1, mix_norm[i], fox_w_in[j], fox_b_f[j], fox_q_norm[j],
                                      fox_k_norm[j], b_hi, b_lo)
            nh = FOX_HEADS_PER_STEP
            w_out = fox_w_out[j]
        else:
            bounded, b_hi, b_lo = _score_bound(mla_q_norm[j], mla_k_norm[j], MLA_QK)
            q, k, v = _mla_pre(x, positions, sh1, sc1, mix_norm[i], mla_w_in[j], mla_q_lat_norm[j],
                               mla_kv_lat_norm[j], mla_w_uq[j], mla_w_ukv[j], mla_q_norm[j], mla_k_norm[j],
                               b_hi, b_lo)
            nh = MLA_HEADS_PER_STEP
            edges = jnp.zeros((x.shape[0], x.shape[1] // TQ, 8, LANES), F32)
            w_out = mla_w_out[j]
        o = _attention(q, k, v, bounded, edges, nh)
        x1, h2, comb = _post(o, x, g1, sh2, sc2, ffn_norm[i], w_out, moe_w_grp[i], moe_b_grp[i],
                             moe_w_rt[i], moe_b_rt[i])
        x = _moe(h2, comb, x1, g2, moe_w_gate[i], moe_w_up[i], moe_w_down[i])
    return x
```

```python
import functools
import math

import numpy as np
import jax
import jax.numpy as jnp
from jax import lax
from jax.experimental import pallas as pl
from jax.experimental.pallas import tpu as pltpu

F32 = jnp.float32
BF16 = jnp.bfloat16

D_MODEL = 1024
HEADS = 16
HEAD_DIM = 64
LANES = 128
PAIRS = HEADS // 2
EPS = 1e-6
MLA_Q_RANK = 384
MLA_KV_RANK = 256
MLA_ROPE = 32
MLA_QK = HEAD_DIM + MLA_ROPE
ROPE_THETA = 10000.0
N_GROUPS = 4
EPG = 4
N_EXPERTS = 16
D_EXPERT = 256

VMEM_LIMIT = 56 * 1024 * 1024
LOG2E = math.log2(math.e)
MAX_FAST_BOUND = 40.0
UNDERFLOW_LOG2 = -150.0
FOX_HEADS_PER_STEP = 2
MLA_HEADS_PER_STEP = 4

TM_PRE = 512
TM_POST = 512
TM_MOE = 1024
MOE_EXPERTS_PER_STEP = 4
TQ = 512
TK = 512
assert TM_PRE == TQ == TK

CUM_HI, CUM_MID, CUM_LO, CUM_ONE = 0, 16, 32, 48


def _params(*sem):
    return pltpu.CompilerParams(dimension_semantics=sem, vmem_limit_bytes=VMEM_LIMIT)


def _sigmoid(x):
    return 1.0 / (1.0 + jnp.exp(-x))


def _modulated_norm(x, gain, scale, shift):
    ms = jnp.mean(x * x, axis=-1, keepdims=True)
    return (x * lax.rsqrt(ms + EPS) * gain) * (1.0 + scale) + shift


def _ada_kernel(c_ref, w_ref, b_ref, o_ref):
    c = c_ref[...]
    act = c * _sigmoid(c)
    o_ref[0] = jnp.dot(act, w_ref[0], preferred_element_type=F32,
                       precision=lax.Precision.HIGHEST) + b_ref[0]


def _ada_modulation(c, w_ada, b_ada):
    depth, d, n = w_ada.shape
    bsz = c.shape[0]
    rows = 8
    tn = 1536
    c_pad = jnp.zeros((rows, d), F32).at[:bsz].set(c)
    out = pl.pallas_call(
        _ada_kernel,
        grid=(depth, n // tn),
        in_specs=[
            pl.BlockSpec((rows, d), lambda i, j: (0, 0)),
            pl.BlockSpec((1, d, tn), lambda i, j: (i, 0, j)),
            pl.BlockSpec((1, 1, tn), lambda i, j: (i, 0, j)),
        ],
        out_specs=pl.BlockSpec((1, rows, tn), lambda i, j: (i, 0, j)),
        out_shape=jax.ShapeDtypeStruct((depth, rows, n), F32),
        compiler_params=_params("arbitrary", "arbitrary"),
        name="ada_mod",
    )(c_pad, w_ada, b_ada.reshape(depth, 1, n))
    return out[:, :bsz]


def _pair_rmsnorm(xp, gain, lo_half):
    sq = xp * xp
    s_all = jnp.sum(sq, axis=-1, keepdims=True)
    s_lo = jnp.sum(jnp.where(lo_half, sq, 0.0), axis=-1, keepdims=True)
    ms = jnp.where(lo_half, s_lo, s_all - s_lo) * (1.0 / HEAD_DIM)
    return xp * lax.rsqrt(ms + EPS) * gain


def _value_blocks(vp, lo_half, lane):
    even = jnp.where(lo_half, vp, jnp.where(lane == HEAD_DIM, 1.0, 0.0))
    odd = jnp.where(lo_half, jnp.where(lane == 0, 1.0, 0.0), vp)
    return even.astype(BF16), odd.astype(BF16)


def _fox_pre_kernel(x_ref, sh_ref, sc_ref, nrm_ref, wqkv_ref, wf_ref, bf_ref, gq_ref, gk_ref,
                    selq_ref, selk_ref, off_ref, q_ref, k_ref, v_ref, edge_ref, carry_ref):
    tm = x_ref.shape[1]

    @pl.when(pl.program_id(1) == 0)
    def _():
        carry_ref[...] = jnp.zeros_like(carry_ref)

    h = _modulated_norm(x_ref[0], nrm_ref[...], sc_ref[0], sh_ref[0])
    hb = h.astype(BF16)
    proj = jnp.dot(hb, wqkv_ref[...], preferred_element_type=F32)
    fl = jnp.dot(hb, wf_ref[...], preferred_element_type=F32) + bf_ref[...]
    logf = jnp.minimum(fl, 0.0) - jnp.log(1.0 + jnp.exp(-jnp.abs(fl)))

    row = lax.broadcasted_iota(jnp.int32, (tm, tm), 0)
    col = lax.broadcasted_iota(jnp.int32, (tm, tm), 1)
    tri = jnp.where(row >= col, 1.0, 0.0).astype(BF16)
    l_hi = logf.astype(BF16)
    l_r = logf - l_hi.astype(F32)
    l_mid = l_r.astype(BF16)
    l_lo = (l_r - l_mid.astype(F32)).astype(BF16)
    cum = (jnp.dot(tri, l_hi, preferred_element_type=F32) + jnp.dot(tri, l_mid, preferred_element_type=F32)
           + jnp.dot(tri, l_lo, preferred_element_type=F32)) + carry_ref[...]
    carry_ref[...] = cum[tm - 1:tm, :]

    lane = lax.broadcasted_iota(jnp.int32, (tm, LANES), 1)
    cum2 = cum * LOG2E
    edge_ref[0, 0, 0:1, :] = cum2[0:1, :]
    edge_ref[0, 0, 1:2, :] = cum2[tm - 1:tm, :]
    edge_ref[0, 0, 2:8, :] = jnp.zeros((6, LANES), F32)
    hi = cum2.astype(BF16).astype(F32)
    r1 = cum2 - hi
    mid = r1.astype(BF16).astype(F32)
    lo = r1 - mid
    parts = jnp.where(lane < CUM_MID, hi,
                      jnp.where(lane < CUM_LO, mid,
                                jnp.where(lane < CUM_ONE, lo, off_ref[...])))
    pb = parts.astype(BF16)
    augq = jnp.dot(pb, selq_ref[...], preferred_element_type=F32)
    augk = jnp.dot(pb, selk_ref[...], preferred_element_type=F32)

    lo_half = lane < HEAD_DIM
    d = D_MODEL
    for p in range(PAIRS):
        qn = _pair_rmsnorm(proj[:, LANES * p:LANES * (p + 1)], gq_ref[...], lo_half) * (HEAD_DIM ** -0.5 * LOG2E)
        kn = _pair_rmsnorm(proj[:, d + LANES * p:d + LANES * (p + 1)], gk_ref[...], lo_half)
        e0, e1, e2 = 2 * LANES * p, 2 * LANES * p + LANES, 2 * LANES * (p + 1)
        q_ref[0, :, e0:e1] = jnp.where(lo_half, qn, augq[:, e0:e1]).astype(BF16)
        q_ref[0, :, e1:e2] = jnp.where(lo_half, augq[:, e1:e2], qn).astype(BF16)
        k_ref[0, :, e0:e1] = jnp.where(lo_half, kn, augk[:, e0:e1]).astype(BF16)
        k_ref[0, :, e1:e2] = jnp.where(lo_half, augk[:, e1:e2], kn).astype(BF16)
        v_even, v_odd = _value_blocks(proj[:, 2 * d + LANES * p:2 * d + LANES * (p + 1)], lo_half, lane)
        v_ref[0, :, e0:e1] = v_even
        v_ref[0, :, e1:e2] = v_odd


def _fox_selectors():
    selq = np.zeros((LANES, HEADS * LANES), np.float32)
    selk = np.zeros((LANES, HEADS * LANES), np.float32)
    for h in range(HEADS):
        base = h * LANES + (HEAD_DIM if h % 2 == 0 else 0)
        for j, src in enumerate((CUM_HI, CUM_MID, CUM_LO)):
            selq[src + h, base + j] = 1.0
            selq[CUM_ONE, base + 3 + j] = 1.0
            selk[CUM_ONE, base + j] = 1.0
            selk[src + h, base + 3 + j] = -1.0
        for j in range(2):
            selq[CUM_ONE + 1 + j, base + 6 + j] = 1.0
            selk[CUM_ONE, base + 6 + j] = 1.0
    return jnp.asarray(selq, BF16), jnp.asarray(selk, BF16)


def _score_bound(q_gain, k_gain, dim):
    bound = (jnp.max(jnp.abs(q_gain)) * jnp.max(jnp.abs(k_gain)) * (LOG2E * math.sqrt(dim))).astype(F32)
    usable = bound <= MAX_FAST_BOUND
    bound = jnp.where(usable, bound, 0.0)
    hi = bound.astype(BF16).astype(F32)
    return usable, hi, bound - hi


def _fox_pre(x, sh, sc, nrm, w_in, b_f, q_norm, k_norm, bound_hi, bound_lo):
    bsz, s, d = x.shape
    tm = TM_PRE
    wqkv = w_in.astype(BF16)
    wf = w_in[:, 3 * d:]
    zpad = jnp.zeros((d, LANES - 3 * HEADS), F32)
    wf3 = jnp.concatenate([wf, wf, wf, zpad], axis=1).astype(BF16)
    bf3 = jnp.concatenate([b_f, b_f, b_f, jnp.zeros((LANES - 3 * HEADS,), F32)]).reshape(1, LANES)
    gq = jnp.tile(q_norm, 2).reshape(1, LANES)
    gk = jnp.tile(k_norm, 2).reshape(1, LANES)
    selq, selk = _fox_selectors()
    off = jnp.zeros((1, LANES), F32).at[0, CUM_ONE].set(1.0)
    off = off.at[0, CUM_ONE + 1].set(-bound_hi).at[0, CUM_ONE + 2].set(-bound_lo)
    full = lambda shape: pl.BlockSpec(shape, lambda b, t: (0,) * len(shape))
    vec = pl.BlockSpec((1, 1, d), lambda b, t: (b, 0, 0))
    hq = HEADS * LANES
    head_blocks = pl.BlockSpec((1, tm, hq), lambda b, t: (b, t, 0))
    return pl.pallas_call(
        _fox_pre_kernel,
        grid=(bsz, s // tm),
        in_specs=[
            pl.BlockSpec((1, tm, d), lambda b, t: (b, t, 0)),
            vec, vec, full((1, d)),
            full((d, 3 * d)), full((d, LANES)), full((1, LANES)), full((1, LANES)), full((1, LANES)),
            full((LANES, hq)), full((LANES, hq)), full((1, LANES)),
        ],
        out_specs=[head_blocks] * 3 + [pl.BlockSpec((1, 1, 8, LANES), lambda b, t: (b, t, 0, 0))],
        out_shape=[jax.ShapeDtypeStruct((bsz, s, hq), BF16)] * 3
        + [jax.ShapeDtypeStruct((bsz, s // tm, 8, LANES), F32)],
        scratch_shapes=[pltpu.VMEM((1, LANES), F32)],
        compiler_params=_params("arbitrary", "arbitrary"),
        name="fox_pre",
    )(x, sh, sc, nrm.reshape(1, d), wqkv, wf3, bf3, gq, gk, selq, selk, off)


def _mla_pre_kernel(x_ref, pos_ref, sh_ref, sc_ref, nrm_ref, win_ref, gql_ref, gkvl_ref,
                    wuq_ref, wuqs_ref, wkn_ref, wv_ref, gq_ref, gqs_ref, gk_ref, invf_ref, qoff_ref, koff_ref,
                    q_ref, k_ref, v_ref):
    tm = x_ref.shape[1]
    h = _modulated_norm(x_ref[0], nrm_ref[...], sc_ref[0], sh_ref[0])
    proj = jnp.dot(h.astype(BF16), win_ref[...], preferred_element_type=F32)
    cq = proj[:, :MLA_Q_RANK]
    ckv = proj[:, MLA_Q_RANK:MLA_Q_RANK + MLA_KV_RANK]
    kr = proj[:, MLA_Q_RANK + MLA_KV_RANK:]
    cqn = cq * lax.rsqrt(jnp.mean(cq * cq, axis=-1, keepdims=True) + EPS) * gql_ref[...]
    ckvn = ckv * lax.rsqrt(jnp.mean(ckv * ckv, axis=-1, keepdims=True) + EPS) * gkvl_ref[...]
    cqb = cqn.astype(BF16)
    ckvb = ckvn.astype(BF16)
    qall = jnp.dot(cqb, wuq_ref[...], preferred_element_type=F32)
    qswap = jnp.dot(cqb, wuqs_ref[...], preferred_element_type=F32)
    knall = jnp.dot(ckvb, wkn_ref[...], preferred_element_type=F32)
    vall = jnp.dot(ckvb, wv_ref[...], preferred_element_type=F32)

    lane = lax.broadcasted_iota(jnp.int32, (tm, LANES), 1)
    lo_half = lane < HEAD_DIM
    for p in range(PAIRS):
        v_even, v_odd = _value_blocks(vall[:, LANES * p:LANES * (p + 1)], lo_half, lane)
        v_ref[0, :, 2 * LANES * p:2 * LANES * p + LANES] = v_even
        v_ref[0, :, 2 * LANES * p + LANES:2 * LANES * (p + 1)] = v_odd
    first_half = lane < HEAD_DIM + MLA_ROPE // 2
    ang = pos_ref[0] * invf_ref[...]
    cosf = jnp.cos(ang)
    sinf = jnp.sin(ang)
    sins = jnp.where(first_half, -sinf, sinf)
    half = MLA_ROPE // 2

    def swap_halves(t):
        return jnp.where(first_half, pltpu.roll(t, LANES - half, axis=1), pltpu.roll(t, half, axis=1))

    gk = gk_ref[...]
    q_cos = gq_ref[...] * cosf
    q_sin = gqs_ref[...] * sins
    k_cos = gk * cosf
    sw_k = swap_halves(kr * gk) * sins
    inv_dim = 1.0 / MLA_QK
    for hd in range(HEADS):
        sl = slice(LANES * hd, LANES * (hd + 1))
        qh = qall[:, sl]
        rq = lax.rsqrt(jnp.sum(qh * qh, axis=-1, keepdims=True) * inv_dim + EPS)
        qf = (qh * q_cos + qswap[:, sl] * q_sin) * rq
        q_ref[0, :, sl] = (qf + qoff_ref[...]).astype(BF16)
        kh = knall[:, sl] + kr
        rk = lax.rsqrt(jnp.sum(kh * kh, axis=-1, keepdims=True) * inv_dim + EPS)
        kf = rk * (kh * k_cos + sw_k)
        k_ref[0, :, sl] = (kf + koff_ref[...]).astype(BF16)


def _mla_pre(x, positions, sh, sc, nrm, w_in, q_lat_norm, kv_lat_norm, w_uq, w_ukv, q_norm, k_norm,
             bound_hi, bound_lo):
    bsz, s, d = x.shape
    tm = TM_PRE
    hq = HEADS * LANES
    lat = MLA_Q_RANK + MLA_KV_RANK
    kr_cols = jnp.zeros((d, LANES), F32).at[:, HEAD_DIM:HEAD_DIM + MLA_ROPE].set(w_in[:, lat:])
    win = jnp.concatenate([w_in[:, :lat], kr_cols], axis=1).astype(BF16)
    half = MLA_ROPE // 2
    r0, r1, r2 = HEAD_DIM, HEAD_DIM + half, HEAD_DIM + MLA_ROPE

    def swap_rotary(t):
        out = jnp.zeros(t.shape[:-1] + (LANES,), F32)
        return out.at[..., r0:r1].set(t[..., r1:r2]).at[..., r1:r2].set(t[..., r0:r1])

    uq = w_uq.reshape(MLA_Q_RANK, HEADS, MLA_QK)
    wuq = jnp.zeros((MLA_Q_RANK, HEADS, LANES), F32).at[:, :, :MLA_QK].set(uq).reshape(
        MLA_Q_RANK, hq).astype(BF16)
    wuqs = swap_rotary(uq).reshape(MLA_Q_RANK, hq).astype(BF16)
    ukv = w_ukv.reshape(MLA_KV_RANK, HEADS, 2 * HEAD_DIM)
    wkn = jnp.zeros((MLA_KV_RANK, HEADS, LANES), F32).at[:, :, :HEAD_DIM].set(
        ukv[:, :, :HEAD_DIM]).reshape(MLA_KV_RANK, hq).astype(BF16)
    wv = ukv[:, :, HEAD_DIM:].reshape(MLA_KV_RANK, HEADS * HEAD_DIM).astype(BF16)
    pad = jnp.zeros((LANES - MLA_QK,), F32)
    gq = jnp.concatenate([q_norm, pad]).reshape(1, LANES) * (MLA_QK ** -0.5 * LOG2E)
    gqs = swap_rotary(gq)
    gk = jnp.concatenate([k_norm, pad]).reshape(1, LANES)
    inv_freq = ROPE_THETA ** (-jnp.arange(0, half, dtype=F32) / half)
    invf = jnp.zeros((LANES,), F32).at[HEAD_DIM:HEAD_DIM + half].set(inv_freq)
    invf = invf.at[HEAD_DIM + half:HEAD_DIM + MLA_ROPE].set(inv_freq).reshape(1, LANES)
    pos = positions.astype(F32).reshape(bsz, s, 1)
    qoff = jnp.zeros((1, LANES), F32).at[0, MLA_QK].set(-bound_hi).at[0, MLA_QK + 1].set(-bound_lo)
    koff = jnp.zeros((1, LANES), F32).at[0, MLA_QK:MLA_QK + 2].set(1.0)
    full = lambda shape: pl.BlockSpec(shape, lambda b, t: (0,) * len(shape))
    vec = pl.BlockSpec((1, 1, d), lambda b, t: (b, 0, 0))
    head_blocks = pl.BlockSpec((1, tm, hq), lambda b, t: (b, t, 0))
    return pl.pallas_call(
        _mla_pre_kernel,
        grid=(bsz, s // tm),
        in_specs=[
            pl.BlockSpec((1, tm, d), lambda b, t: (b, t, 0)),
            pl.BlockSpec((1, tm, 1), lambda b, t: (b, t, 0)),
            vec, vec, full((1, d)),
            full((d, lat + LANES)), full((1, MLA_Q_RANK)), full((1, MLA_KV_RANK)),
            full((MLA_Q_RANK, hq)), full((MLA_Q_RANK, hq)), full((MLA_KV_RANK, hq)), full((MLA_KV_RANK, d)),
            full((1, LANES)), full((1, LANES)), full((1, LANES)), full((1, LANES)), full((1, LANES)),
            full((1, LANES)),
        ],
        out_specs=[head_blocks] * 3,
        out_shape=[jax.ShapeDtypeStruct((bsz, s, hq), BF16)] * 3,
        compiler_params=_params("arbitrary", "arbitrary"),
        name="mla_pre",
    )(x, pos, sh, sc, nrm.reshape(1, d), win, q_lat_norm.reshape(1, -1), kv_lat_norm.reshape(1, -1),
      wuq, wuqs, wkn, wv, gq, gqs, gk, invf, qoff, koff)


def _attn_tiles(q_ref, k_ref, v_ref, hh, j):
    hs = slice(LANES * hh, LANES * (hh + 1))
    start = pl.multiple_of(j * TK, TK)
    return k_ref[0, pl.ds(start, TK), hs], v_ref[0, pl.ds(start, TK), hs]


def _causal_mask(tq):
    row = lax.broadcasted_iota(jnp.int32, (tq, TK), 0)
    col = lax.broadcasted_iota(jnp.int32, (tq, TK), 1)
    return row >= col


def _attn_online_kernel(q_ref, k_ref, v_ref, o_ref):
    tq = q_ref.shape[1]
    qi = pl.program_id(2)
    causal = _causal_mask(tq)
    qs = [q_ref[0, :, LANES * hh:LANES * (hh + 1)] for hh in range(2)]

    def step(j, carry, masked):
        new = []
        for hh in range(2):
            m, l, acc = carry[hh]
            k, v = _attn_tiles(q_ref, k_ref, v_ref, hh, j)
            s = lax.dot_general(qs[hh], k, (((1,), (1,)), ((), ())), preferred_element_type=F32)
            if masked:
                s = jnp.where(causal, s, -jnp.inf)
            m_new = jnp.maximum(m, jnp.max(s, axis=-1, keepdims=True))
            alpha = jnp.exp2(m - m_new)
            p = jnp.exp2(s - m_new)
            l_new = alpha * l + jnp.sum(p, axis=-1, keepdims=True)
            acc_new = alpha * acc + jnp.dot(p.astype(BF16), v, preferred_element_type=F32)
            new.append((m_new, l_new, acc_new))
        return tuple(new)

    one = (jnp.full((tq, 1), -jnp.inf, F32), jnp.zeros((tq, 1), F32), jnp.zeros((tq, LANES), F32))
    carry = lax.fori_loop(0, qi, functools.partial(step, masked=False), (one, one))
    (_, l0, a0), (_, l1, a1) = step(qi, carry, True)
    lane = lax.broadcasted_iota(jnp.int32, (tq, LANES), 1)
    o_ref[0] = jnp.where(lane < HEAD_DIM, a0 / l0, a1 / l1).astype(BF16)


def _first_key_tile(edge_ref, qi, group, nh):
    first = edge_ref[0, qi, 0:1, :]
    skipped = jnp.zeros((1, LANES), F32)
    for j in range(edge_ref.shape[1]):
        skipped += jnp.where(first - edge_ref[0, j, 1:2, :] <= UNDERFLOW_LOG2, 1.0, 0.0)
    lane = lax.broadcasted_iota(jnp.int32, (1, LANES), 1)
    in_group = jnp.logical_and(lane >= group * nh, lane < (group + 1) * nh)
    return jnp.min(jnp.where(in_group, skipped, float(edge_ref.shape[1]))).astype(jnp.int32)


def _attn_bounded_kernel(edge_ref, q_ref, k_ref, v_ref, o_ref):
    tq = q_ref.shape[1]
    nh = q_ref.shape[2] // LANES
    qi = pl.program_id(2)
    j0 = _first_key_tile(edge_ref, qi, pl.program_id(1), nh)
    causal = _causal_mask(tq)
    qs = [q_ref[0, :, LANES * hh:LANES * (hh + 1)] for hh in range(nh)]

    def step(j, accs, masked):
        new = []
        for hh in range(nh):
            k, v = _attn_tiles(q_ref, k_ref, v_ref, hh, j)
            s = lax.dot_general(qs[hh], k, (((1,), (1,)), ((), ())), preferred_element_type=F32)
            if masked:
                s = jnp.where(causal, s, -jnp.inf)
            p = jnp.exp2(s).astype(BF16)
            new.append(accs[hh] + jnp.dot(p, v, preferred_element_type=F32))
        return tuple(new)

    def steps(n, base):
        def body(jj, accs):
            for u in range(n):
                accs = step(base + n * jj + u, accs, False)
            return accs
        return body

    def finish(accs):
        accs = step(qi, accs, True)
        lane = lax.broadcasted_iota(jnp.int32, (tq, LANES), 1)
        for p in range(nh // 2):
            a0, a1 = accs[2 * p], accs[2 * p + 1]
            o_ref[0, :, LANES * p:LANES * (p + 1)] = jnp.where(
                lane < HEAD_DIM, a0 / a0[:, HEAD_DIM:HEAD_DIM + 1], a1 / a1[:, 0:1]).astype(BF16)

    n = qi - j0
    accs = (jnp.zeros((tq, LANES), F32),) * nh
    accs = lax.fori_loop(0, n >> 2, steps(4, j0), accs)
    accs = lax.fori_loop(0, (n >> 1) & 1, steps(2, j0 + ((n >> 2) << 2)), accs)

    @pl.when((n & 1) == 1)
    def _():
        finish(step(qi - 1, accs, False))

    @pl.when((n & 1) == 0)
    def _():
        finish(accs)


def _attention(q, k, v, bounded, edges, heads_per_step):
    bsz, s, _ = q.shape
    assert TQ == TK

    def call(body, name, nh, with_edges):
        w = nh * LANES
        in_specs = [
            pl.BlockSpec((1, TQ, w), lambda b, g, i: (b, i, g)),
            pl.BlockSpec((1, s, w), lambda b, g, i: (b, 0, g)),
            pl.BlockSpec((1, s, w), lambda b, g, i: (b, 0, g)),
        ]
        if with_edges:
            in_specs.insert(0, pl.BlockSpec((1,) + edges.shape[1:], lambda b, g, i: (b, 0, 0, 0)))
        return pl.pallas_call(
            body,
            grid=(bsz, HEADS // nh, s // TQ),
            in_specs=in_specs,
            out_specs=pl.BlockSpec((1, TQ, w // 2), lambda b, g, i: (b, i, g)),
            out_shape=jax.ShapeDtypeStruct((bsz, s, PAIRS * LANES), BF16),
            compiler_params=_params("arbitrary", "arbitrary", "arbitrary"),
            name=name,
        )

    fast = call(_attn_bounded_kernel, "attn_bounded", heads_per_step, True)
    online = call(_attn_online_kernel, "attn_online", 2, False)
    return lax.cond(bounded, fast, lambda e, *qkv: online(*qkv), edges, q, k, v)


def _post_kernel(o_ref, x_ref, g1_ref, sh_ref, sc_ref, nrm_ref, wout_ref, wrhl_ref, br_ref,
                 x1_ref, h2_ref, comb_ref):
    tm = x_ref.shape[1]
    y = jnp.dot(o_ref[0], wout_ref[...], preferred_element_type=F32)
    x1 = x_ref[0] + g1_ref[0] * y
    x1_ref[0] = x1
    h2 = _modulated_norm(x1, nrm_ref[...], sc_ref[0], sh_ref[0])
    h_hi = h2.astype(BF16)
    h2_ref[0] = h_hi
    h_lo = (h2 - h_hi.astype(F32)).astype(BF16)
    hw = jnp.dot(h_hi, wrhl_ref[...], preferred_element_type=F32)
    lg = (hw[:, :LANES] + hw[:, LANES:]
          + jnp.dot(h_lo, wrhl_ref[:, :LANES], preferred_element_type=F32)) + br_ref[...]
    lane = lax.broadcasted_iota(jnp.int32, (tm, LANES), 1)
    lanef = lane.astype(F32)
    neg = -jnp.inf
    is_g = jnp.logical_and(lane >= N_EXPERTS, lane < N_EXPERTS + N_GROUPS)
    gl = jnp.where(is_g, lg, neg)
    gmax = jnp.max(gl, axis=-1, keepdims=True)
    gsum = jnp.sum(jnp.exp(gl - gmax), axis=-1, keepdims=True)
    g_top = 1.0 / gsum
    gidx = jnp.min(jnp.where(gl == gmax, lanef - N_EXPERTS, 1e9), axis=-1, keepdims=True)
    lane_grp = (lane >> 2).astype(F32)
    in_grp = jnp.logical_and(lane < N_EXPERTS, lane_grp == gidx)
    el = jnp.where(in_grp, lg, neg)
    e1 = jnp.max(el, axis=-1, keepdims=True)
    i1 = jnp.min(jnp.where(el == e1, lanef, 1e9), axis=-1, keepdims=True)
    el2 = jnp.where(lanef == i1, neg, el)
    e2 = jnp.max(el2, axis=-1, keepdims=True)
    i2 = jnp.min(jnp.where(el2 == e2, lanef, 1e9), axis=-1, keepdims=True)
    t = jnp.exp(e2 - e1)
    w1 = g_top / (1.0 + t)
    w2 = g_top * t / (1.0 + t)
    comb_ref[0] = jnp.where(lanef == i1, w1, jnp.where(lanef == i2, w2, 0.0))


def _post(o, x, g1, sh, sc, nrm, w_out, w_grp, b_grp, w_rt, b_rt):
    bsz, s, d = x.shape
    tm = TM_POST
    wr = jnp.zeros((d, LANES), F32).at[:, :N_EXPERTS].set(w_rt)
    wr = wr.at[:, N_EXPERTS:N_EXPERTS + N_GROUPS].set(w_grp)
    wrh = wr.astype(BF16)
    wrhl = jnp.concatenate([wrh, (wr - wrh.astype(F32)).astype(BF16)], axis=1)
    br =jnp.zeros((1, LANES), F32).at[0, :N_EXPERTS].set(b_rt)
    br = br.at[0, N_EXPERTS:N_EXPERTS + N_GROUPS].set(b_grp)
    full = lambda shape: pl.BlockSpec(shape, lambda b, t: (0,) * len(shape))
    vec = pl.BlockSpec((1, 1, d), lambda b, t: (b, 0, 0))
    tile = lambda w: pl.BlockSpec((1, tm, w), lambda b, t: (b, t, 0))
    return pl.pallas_call(
        _post_kernel,
        grid=(bsz, s // tm),
        in_specs=[tile(d), tile(d), vec, vec, vec, full((1, d)), full((d, d)),
                  full((d, 2 * LANES)), full((1, LANES))],
        out_specs=[tile(d), tile(d), tile(LANES)],
        out_shape=[
            jax.ShapeDtypeStruct((bsz, s, d), F32),
            jax.ShapeDtypeStruct((bsz, s, d), BF16),
            jax.ShapeDtypeStruct((bsz, s, LANES), F32),
        ],
        compiler_params=_params("arbitrary", "arbitrary"),
        name="post_router",
    )(o, x, g1, sh, sc, nrm.reshape(1, d), w_out.astype(BF16), wrhl, br)


def _moe_kernel(h_ref, comb_ref, x1_ref, g2_ref, wg_ref, wu_ref, wd_ref, out_ref, acc_ref):
    tm = h_ref.shape[0]
    n_block = wg_ref.shape[1]
    step = pl.program_id(1)
    last = N_EXPERTS // n_block - 1
    assert last >= 1
    h = h_ref[...]
    comb = comb_ref[...]
    lane = lax.broadcasted_iota(jnp.int32, (tm, LANES), 1)
    total = None
    for j in range(n_block):
        a = jnp.dot(h, wg_ref[0, j], preferred_element_type=F32)
        u = jnp.dot(h, wu_ref[0, j], preferred_element_type=F32)
        cw = jnp.sum(jnp.where(lane == step * n_block + j, comb, 0.0), axis=-1, keepdims=True)
        hid = (a * _sigmoid(a)) * u * cw
        part = jnp.dot(hid.astype(BF16), wd_ref[0, j], preferred_element_type=F32)
        total = part if total is None else total + part

    @pl.when(step == 0)
    def _():
        acc_ref[...] = total

    @pl.when(jnp.logical_and(step > 0, step < last))
    def _():
        acc_ref[...] += total

    @pl.when(step == last)
    def _():
        out_ref[...] = x1_ref[...] + g2_ref[0] * (acc_ref[...] + total)


def _moe(h2, comb, x1, g2, w_gate, w_up, w_down, layer):
    bsz, s, d = x1.shape
    t = bsz * s
    tm = TM_MOE
    per_batch = s // tm
    ne = MOE_EXPERTS_PER_STEP
    out = pl.pallas_call(
        _moe_kernel,
        grid=(t // tm, N_EXPERTS // MOE_EXPERTS_PER_STEP),
        in_specs=[
            pl.BlockSpec((tm, d), lambda i, e: (i, 0)),
            pl.BlockSpec((tm, LANES), lambda i, e: (i, 0)),
            pl.BlockSpec((tm, d), lambda i, e: (i, 0)),
            pl.BlockSpec((1, 1, d), lambda i, e: (i // per_batch, 0, 0)),
            pl.BlockSpec((1, ne, d, D_EXPERT), lambda i, e: (layer, e, 0, 0)),
            pl.BlockSpec((1, ne, d, D_EXPERT), lambda i, e: (layer, e, 0, 0)),
            pl.BlockSpec((1, ne, D_EXPERT, d), lambda i, e: (layer, e, 0, 0)),
        ],
        out_specs=pl.BlockSpec((tm, d), lambda i, e: (i, 0)),
        out_shape=jax.ShapeDtypeStruct((t, d), F32),
        scratch_shapes=[pltpu.VMEM((tm, d), F32)],
        compiler_params=_params("arbitrary", "arbitrary"),
        name="moe_dense",
    )(h2.reshape(t, d), comb.reshape(t, LANES), x1.reshape(t, d), g2, w_gate, w_up, w_down)
    return out.reshape(bsz, s, d)


def kernel(x, c, positions, w_ada, b_ada, mix_norm, ffn_norm, fox_w_in, fox_b_f, fox_q_norm, fox_k_norm,
           fox_w_out, mla_w_in, mla_q_lat_norm, mla_kv_lat_norm, mla_w_uq, mla_w_ukv, mla_q_norm, mla_k_norm,
           mla_w_out, moe_w_grp, moe_b_grp, moe_w_rt, moe_b_rt, moe_w_gate, moe_w_up, moe_w_down):
    depth = w_ada.shape[0]
    d = x.shape[-1]
    mod = _ada_modulation(c, w_ada, b_ada)
    wg, wu, wd = moe_w_gate.astype(BF16), moe_w_up.astype(BF16), moe_w_down.astype(BF16)
    for i in range(depth):
        sh1, sc1, g1, sh2, sc2, g2 = [mod[i, :, None, d * n:d * (n + 1)] for n in range(6)]
        j = i // 2
        if i % 2 == 0:
            bounded, b_hi, b_lo = _score_bound(fox_q_norm[j], fox_k_norm[j], HEAD_DIM)
            q, k, v, edges = _fox_pre(x, sh1, sc1, mix_norm[i], fox_w_in[j], fox_b_f[j], fox_q_norm[j],
                                      fox_k_norm[j], b_hi, b_lo)
            nh = FOX_HEADS_PER_STEP
            w_out = fox_w_out[j]
        else:
            bounded, b_hi, b_lo = _score_bound(mla_q_norm[j], mla_k_norm[j], MLA_QK)
            q, k, v = _mla_pre(x, positions, sh1, sc1, mix_norm[i], mla_w_in[j], mla_q_lat_norm[j],
                               mla_kv_lat_norm[j], mla_w_uq[j], mla_w_ukv[j], mla_q_norm[j], mla_k_norm[j],
                               b_hi, b_lo)
            nh = MLA_HEADS_PER_STEP
            edges = jnp.zeros((x.shape[0], x.shape[1] // TQ, 8, LANES), F32)
            w_out = mla_w_out[j]
        o = _attention(q, k, v, bounded, edges, nh)
        x1, h2, comb = _post(o, x, g1, sh2, sc2, ffn_norm[i], w_out, moe_w_grp[i], moe_b_grp[i],
                             moe_w_rt[i], moe_b_rt[i])
        x = _moe(h2, comb, x1, g2, wg, wu, wd, i)
    return x
```

```python
import functools
import math

import numpy as np
import jax
import jax.numpy as jnp
from jax import lax
from jax.experimental import pallas as pl
from jax.experimental.pallas import tpu as pltpu

F32 = jnp.float32
BF16 = jnp.bfloat16

D_MODEL = 1024
HEADS = 16
HEAD_DIM = 64
LANES = 128
PAIRS = HEADS // 2
EPS = 1e-6
MLA_Q_RANK = 384
MLA_KV_RANK = 256
MLA_ROPE = 32
MLA_QK = HEAD_DIM + MLA_ROPE
ROPE_THETA = 10000.0
N_GROUPS = 4
EPG = 4
N_EXPERTS = 16
D_EXPERT = 256

VMEM_LIMIT = 56 * 1024 * 1024
LOG2E = math.log2(math.e)
MAX_FAST_BOUND = 40.0
UNDERFLOW_LOG2 = -150.0
FOX_HEADS_PER_STEP = 2
MLA_HEADS_PER_STEP = 4

TM_PRE = 512
TM_POST = 512
TM_MOE = 1024
MOE_EXPERTS_PER_STEP = 8
TQ = 512
TK = 512
assert TM_PRE == TQ == TK

CUM_HI, CUM_MID, CUM_LO, CUM_ONE = 0, 16, 32, 48


def _params(*sem):
    return pltpu.CompilerParams(dimension_semantics=sem, vmem_limit_bytes=VMEM_LIMIT)


def _sigmoid(x):
    return 1.0 / (1.0 + jnp.exp(-x))


def _modulated_norm(x, gain, scale, shift):
    ms = jnp.mean(x * x, axis=-1, keepdims=True)
    return (x * lax.rsqrt(ms + EPS) * gain) * (1.0 + scale) + shift


def _ada_kernel(c_ref, w_ref, b_ref, o_ref):
    c = c_ref[...]
    act = c * _sigmoid(c)
    o_ref[0] = jnp.dot(act, w_ref[0], preferred_element_type=F32,
                       precision=lax.Precision.HIGHEST) + b_ref[0]


def _ada_modulation(c, w_ada, b_ada):
    depth, d, n = w_ada.shape
    bsz = c.shape[0]
    rows = 8
    tn = 1536
    c_pad = jnp.zeros((rows, d), F32).at[:bsz].set(c)
    out = pl.pallas_call(
        _ada_kernel,
        grid=(depth, n // tn),
        in_specs=[
            pl.BlockSpec((rows, d), lambda i, j: (0, 0)),
            pl.BlockSpec((1, d, tn), lambda i, j: (i, 0, j)),
            pl.BlockSpec((1, 1, tn), lambda i, j: (i, 0, j)),
        ],
        out_specs=pl.BlockSpec((1, rows, tn), lambda i, j: (i, 0, j)),
        out_shape=jax.ShapeDtypeStruct((depth, rows, n), F32),
        compiler_params=_params("arbitrary", "arbitrary"),
        name="ada_mod",
    )(c_pad, w_ada, b_ada.reshape(depth, 1, n))
    return out[:, :bsz]


def _pair_rmsnorm(xp, gain, lo_half):
    sq = xp * xp
    s_all = jnp.sum(sq, axis=-1, keepdims=True)
    s_lo = jnp.sum(jnp.where(lo_half, sq, 0.0), axis=-1, keepdims=True)
    ms = jnp.where(lo_half, s_lo, s_all - s_lo) * (1.0 / HEAD_DIM)
    return xp * lax.rsqrt(ms + EPS) * gain


def _value_blocks(vp, lo_half, lane):
    even = jnp.where(lo_half, vp, jnp.where(lane == HEAD_DIM, 1.0, 0.0))
    odd = jnp.where(lo_half, jnp.where(lane == 0, 1.0, 0.0), vp)
    return even.astype(BF16), odd.astype(BF16)


def _fox_pre_kernel(x_ref, sh_ref, sc_ref, nrm_ref, wqkv_ref, wf_ref, bf_ref, gq_ref, gk_ref,
                    selq_ref, selk_ref, off_ref, q_ref, k_ref, v_ref, edge_ref, carry_ref):
    tm = x_ref.shape[1]

    @pl.when(pl.program_id(1) == 0)
    def _():
        carry_ref[...] = jnp.zeros_like(carry_ref)

    h = _modulated_norm(x_ref[0], nrm_ref[...], sc_ref[0], sh_ref[0])
    hb = h.astype(BF16)
    proj = jnp.dot(hb, wqkv_ref[...], preferred_element_type=F32)
    fl = jnp.dot(hb, wf_ref[...], preferred_element_type=F32) + bf_ref[...]
    logf = jnp.minimum(fl, 0.0) - jnp.log(1.0 + jnp.exp(-jnp.abs(fl)))

    row = lax.broadcasted_iota(jnp.int32, (tm, tm), 0)
    col = lax.broadcasted_iota(jnp.int32, (tm, tm), 1)
    tri = jnp.where(row >= col, 1.0, 0.0).astype(BF16)
    l_hi = logf.astype(BF16)
    l_r = logf - l_hi.astype(F32)
    l_mid = l_r.astype(BF16)
    l_lo = (l_r - l_mid.astype(F32)).astype(BF16)
    cum = (jnp.dot(tri, l_hi, preferred_element_type=F32) + jnp.dot(tri, l_mid, preferred_element_type=F32)
           + jnp.dot(tri, l_lo, preferred_element_type=F32)) + carry_ref[...]
    carry_ref[...] = cum[tm - 1:tm, :]

    lane = lax.broadcasted_iota(jnp.int32, (tm, LANES), 1)
    cum2 = cum * LOG2E
    edge_ref[0, 0, 0:1, :] = cum2[0:1, :]
    edge_ref[0, 0, 1:2, :] = cum2[tm - 1:tm, :]
    edge_ref[0, 0, 2:8, :] = jnp.zeros((6, LANES), F32)
    hi = cum2.astype(BF16).astype(F32)
    r1 = cum2 - hi
    mid = r1.astype(BF16).astype(F32)
    lo = r1 - mid
    parts = jnp.where(lane < CUM_MID, hi,
                      jnp.where(lane < CUM_LO, mid,
                                jnp.where(lane < CUM_ONE, lo, off_ref[...])))
    pb = parts.astype(BF16)
    augq = jnp.dot(pb, selq_ref[...], preferred_element_type=F32)
    augk = jnp.dot(pb, selk_ref[...], preferred_element_type=F32)

    lo_half = lane < HEAD_DIM
    d = D_MODEL
    for p in range(PAIRS):
        qn = _pair_rmsnorm(proj[:, LANES * p:LANES * (p + 1)], gq_ref[...], lo_half) * (HEAD_DIM ** -0.5 * LOG2E)
        kn = _pair_rmsnorm(proj[:, d + LANES * p:d + LANES * (p + 1)], gk_ref[...], lo_half)
        e0, e1, e2 = 2 * LANES * p, 2 * LANES * p + LANES, 2 * LANES * (p + 1)
        q_ref[0, :, e0:e1] = jnp.where(lo_half, qn, augq[:, e0:e1]).astype(BF16)
        q_ref[0, :, e1:e2] = jnp.where(lo_half, augq[:, e1:e2], qn).astype(BF16)
        k_ref[0, :, e0:e1] = jnp.where(lo_half, kn, augk[:, e0:e1]).astype(BF16)
        k_ref[0, :, e1:e2] = jnp.where(lo_half, augk[:, e1:e2], kn).astype(BF16)
        v_even, v_odd = _value_blocks(proj[:, 2 * d + LANES * p:2 * d + LANES * (p + 1)], lo_half, lane)
        v_ref[0, :, e0:e1] = v_even
        v_ref[0, :, e1:e2] = v_odd


def _fox_selectors():
    selq = np.zeros((LANES, HEADS * LANES), np.float32)
    selk = np.zeros((LANES, HEADS * LANES), np.float32)
    for h in range(HEADS):
        base = h * LANES + (HEAD_DIM if h % 2 == 0 else 0)
        for j, src in enumerate((CUM_HI, CUM_MID, CUM_LO)):
            selq[src + h, base + j] = 1.0
            selq[CUM_ONE, base + 3 + j] = 1.0
            selk[CUM_ONE, base + j] = 1.0
            selk[src + h, base + 3 + j] = -1.0
        for j in range(2):
            selq[CUM_ONE + 1 + j, base + 6 + j] = 1.0
            selk[CUM_ONE, base + 6 + j] = 1.0
    return jnp.asarray(selq, BF16), jnp.asarray(selk, BF16)


def _score_bound(q_gain, k_gain, dim):
    bound = (jnp.max(jnp.abs(q_gain)) * jnp.max(jnp.abs(k_gain)) * (LOG2E * math.sqrt(dim))).astype(F32)
    usable = bound <= MAX_FAST_BOUND
    bound = jnp.where(usable, bound, 0.0)
    hi = bound.astype(BF16).astype(F32)
    return usable, hi, bound - hi


def _fox_pre(x, sh, sc, nrm, w_in, b_f, q_norm, k_norm, bound_hi, bound_lo):
    bsz, s, d = x.shape
    tm = TM_PRE
    wqkv = w_in.astype(BF16)
    wf = w_in[:, 3 * d:]
    zpad = jnp.zeros((d, LANES - 3 * HEADS), F32)
    wf3 = jnp.concatenate([wf, wf, wf, zpad], axis=1).astype(BF16)
    bf3 = jnp.concatenate([b_f, b_f, b_f, jnp.zeros((LANES - 3 * HEADS,), F32)]).reshape(1, LANES)
    gq = jnp.tile(q_norm, 2).reshape(1, LANES)
    gk = jnp.tile(k_norm, 2).reshape(1, LANES)
    selq, selk = _fox_selectors()
    off = jnp.zeros((1, LANES), F32).at[0, CUM_ONE].set(1.0)
    off = off.at[0, CUM_ONE + 1].set(-bound_hi).at[0, CUM_ONE + 2].set(-bound_lo)
    full = lambda shape: pl.BlockSpec(shape, lambda b, t: (0,) * len(shape))
    vec = pl.BlockSpec((1, 1, d), lambda b, t: (b, 0, 0))
    hq = HEADS * LANES
    head_blocks = pl.BlockSpec((1, tm, hq), lambda b, t: (b, t, 0))
    return pl.pallas_call(
        _fox_pre_kernel,
        grid=(bsz, s // tm),
        in_specs=[
            pl.BlockSpec((1, tm, d), lambda b, t: (b, t, 0)),
            vec, vec, full((1, d)),
            full((d, 3 * d)), full((d, LANES)), full((1, LANES)), full((1, LANES)), full((1, LANES)),
            full((LANES, hq)), full((LANES, hq)), full((1, LANES)),
        ],
        out_specs=[head_blocks] * 3 + [pl.BlockSpec((1, 1, 8, LANES), lambda b, t: (b, t, 0, 0))],
        out_shape=[jax.ShapeDtypeStruct((bsz, s, hq), BF16)] * 3
        + [jax.ShapeDtypeStruct((bsz, s // tm, 8, LANES), F32)],
        scratch_shapes=[pltpu.VMEM((1, LANES), F32)],
        compiler_params=_params("arbitrary", "arbitrary"),
        name="fox_pre",
    )(x, sh, sc, nrm.reshape(1, d), wqkv, wf3, bf3, gq, gk, selq, selk, off)


def _mla_pre_kernel(x_ref, pos_ref, sh_ref, sc_ref, nrm_ref, win_ref, gql_ref, gkvl_ref,
                    wuq_ref, wuqs_ref, wkn_ref, wv_ref, gq_ref, gqs_ref, gk_ref, invf_ref, qoff_ref, koff_ref,
                    q_ref, k_ref, v_ref):
    tm = x_ref.shape[1]
    h = _modulated_norm(x_ref[0], nrm_ref[...], sc_ref[0], sh_ref[0])
    proj = jnp.dot(h.astype(BF16), win_ref[...], preferred_element_type=F32)
    cq = proj[:, :MLA_Q_RANK]
    ckv = proj[:, MLA_Q_RANK:MLA_Q_RANK + MLA_KV_RANK]
    kr = proj[:, MLA_Q_RANK + MLA_KV_RANK:]
    cqn = cq * lax.rsqrt(jnp.mean(cq * cq, axis=-1, keepdims=True) + EPS) * gql_ref[...]
    ckvn = ckv * lax.rsqrt(jnp.mean(ckv * ckv, axis=-1, keepdims=True) + EPS) * gkvl_ref[...]
    cqb = cqn.astype(BF16)
    ckvb = ckvn.astype(BF16)
    qall = jnp.dot(cqb, wuq_ref[...], preferred_element_type=F32)
    qswap = jnp.dot(cqb, wuqs_ref[...], preferred_element_type=F32)
    knall = jnp.dot(ckvb, wkn_ref[...], preferred_element_type=F32)
    vall = jnp.dot(ckvb, wv_ref[...], preferred_element_type=F32)

    lane = lax.broadcasted_iota(jnp.int32, (tm, LANES), 1)
    lo_half = lane < HEAD_DIM
    for p in range(PAIRS):
        v_even, v_odd = _value_blocks(vall[:, LANES * p:LANES * (p + 1)], lo_half, lane)
        v_ref[0, :, 2 * LANES * p:2 * LANES * p + LANES] = v_even
        v_ref[0, :, 2 * LANES * p + LANES:2 * LANES * (p + 1)] = v_odd
    first_half = lane < HEAD_DIM + MLA_ROPE // 2
    ang = pos_ref[0] * invf_ref[...]
    cosf = jnp.cos(ang)
    sinf = jnp.sin(ang)
    sins = jnp.where(first_half, -sinf, sinf)
    half = MLA_ROPE // 2

    def swap_halves(t):
        return jnp.where(first_half, pltpu.roll(t, LANES - half, axis=1), pltpu.roll(t, half, axis=1))

    gk = gk_ref[...]
    q_cos = gq_ref[...] * cosf
    q_sin = gqs_ref[...] * sins
    k_cos = gk * cosf
    sw_k = swap_halves(kr * gk) * sins
    inv_dim = 1.0 / MLA_QK
    for hd in range(HEADS):
        sl = slice(LANES * hd, LANES * (hd + 1))
        qh = qall[:, sl]
        rq = lax.rsqrt(jnp.sum(qh * qh, axis=-1, keepdims=True) * inv_dim + EPS)
        qf = (qh * q_cos + qswap[:, sl] * q_sin) * rq
        q_ref[0, :, sl] = (qf + qoff_ref[...]).astype(BF16)
        kh = knall[:, sl] + kr
        rk = lax.rsqrt(jnp.sum(kh * kh, axis=-1, keepdims=True) * inv_dim + EPS)
        kf = rk * (kh * k_cos + sw_k)
        k_ref[0, :, sl] = (kf + koff_ref[...]).astype(BF16)


def _mla_pre(x, positions, sh, sc, nrm, w_in, q_lat_norm, kv_lat_norm, w_uq, w_ukv, q_norm, k_norm,
             bound_hi, bound_lo):
    bsz, s, d = x.shape
    tm = TM_PRE
    hq = HEADS * LANES
    lat = MLA_Q_RANK + MLA_KV_RANK
    kr_cols = jnp.zeros((d, LANES), F32).at[:, HEAD_DIM:HEAD_DIM + MLA_ROPE].set(w_in[:, lat:])
    win = jnp.concatenate([w_in[:, :lat], kr_cols], axis=1).astype(BF16)
    half = MLA_ROPE // 2
    r0, r1, r2 = HEAD_DIM, HEAD_DIM + half, HEAD_DIM + MLA_ROPE

    def swap_rotary(t):
        out = jnp.zeros(t.shape[:-1] + (LANES,), F32)
        return out.at[..., r0:r1].set(t[..., r1:r2]).at[..., r1:r2].set(t[..., r0:r1])

    uq = w_uq.reshape(MLA_Q_RANK, HEADS, MLA_QK)
    wuq = jnp.zeros((MLA_Q_RANK, HEADS, LANES), F32).at[:, :, :MLA_QK].set(uq).reshape(
        MLA_Q_RANK, hq).astype(BF16)
    wuqs = swap_rotary(uq).reshape(MLA_Q_RANK, hq).astype(BF16)
    ukv = w_ukv.reshape(MLA_KV_RANK, HEADS, 2 * HEAD_DIM)
    wkn = jnp.zeros((MLA_KV_RANK, HEADS, LANES), F32).at[:, :, :HEAD_DIM].set(
        ukv[:, :, :HEAD_DIM]).reshape(MLA_KV_RANK, hq).astype(BF16)
    wv = ukv[:, :, HEAD_DIM:].reshape(MLA_KV_RANK, HEADS * HEAD_DIM).astype(BF16)
    pad = jnp.zeros((LANES - MLA_QK,), F32)
    gq = jnp.concatenate([q_norm, pad]).reshape(1, LANES) * (MLA_QK ** -0.5 * LOG2E)
    gqs = swap_rotary(gq)
    gk = jnp.concatenate([k_norm, pad]).reshape(1, LANES)
    inv_freq = ROPE_THETA ** (-jnp.arange(0, half, dtype=F32) / half)
    invf = jnp.zeros((LANES,), F32).at[HEAD_DIM:HEAD_DIM + half].set(inv_freq)
    invf = invf.at[HEAD_DIM + half:HEAD_DIM + MLA_ROPE].set(inv_freq).reshape(1, LANES)
    pos = positions.astype(F32).reshape(bsz, s, 1)
    qoff = jnp.zeros((1, LANES), F32).at[0, MLA_QK].set(-bound_hi).at[0, MLA_QK + 1].set(-bound_lo)
    koff = jnp.zeros((1, LANES), F32).at[0, MLA_QK:MLA_QK + 2].set(1.0)
    full = lambda shape: pl.BlockSpec(shape, lambda b, t: (0,) * len(shape))
    vec = pl.BlockSpec((1, 1, d), lambda b, t: (b, 0, 0))
    head_blocks = pl.BlockSpec((1, tm, hq), lambda b, t: (b, t, 0))
    return pl.pallas_call(
        _mla_pre_kernel,
        grid=(bsz, s // tm),
        in_specs=[
            pl.BlockSpec((1, tm, d), lambda b, t: (b, t, 0)),
            pl.BlockSpec((1, tm, 1), lambda b, t: (b, t, 0)),
            vec, vec, full((1, d)),
            full((d, lat + LANES)), full((1, MLA_Q_RANK)), full((1, MLA_KV_RANK)),
            full((MLA_Q_RANK, hq)), full((MLA_Q_RANK, hq)), full((MLA_KV_RANK, hq)), full((MLA_KV_RANK, d)),
            full((1, LANES)), full((1, LANES)), full((1, LANES)), full((1, LANES)), full((1, LANES)),
            full((1, LANES)),
        ],
        out_specs=[head_blocks] * 3,
        out_shape=[jax.ShapeDtypeStruct((bsz, s, hq), BF16)] * 3,
        compiler_params=_params("arbitrary", "arbitrary"),
        name="mla_pre",
    )(x, pos, sh, sc, nrm.reshape(1, d), win, q_lat_norm.reshape(1, -1), kv_lat_norm.reshape(1, -1),
      wuq, wuqs, wkn, wv, gq, gqs, gk, invf, qoff, koff)


def _attn_tiles(q_ref, k_ref, v_ref, hh, j):
    hs = slice(LANES * hh, LANES * (hh + 1))
    start = pl.multiple_of(j * TK, TK)
    return k_ref[0, pl.ds(start, TK), hs], v_ref[0, pl.ds(start, TK), hs]


def _causal_mask(tq):
    row = lax.broadcasted_iota(jnp.int32, (tq, TK), 0)
    col = lax.broadcasted_iota(jnp.int32, (tq, TK), 1)
    return row >= col


def _attn_online_kernel(q_ref, k_ref, v_ref, o_ref):
    tq = q_ref.shape[1]
    qi = pl.program_id(2)
    causal = _causal_mask(tq)
    qs = [q_ref[0, :, LANES * hh:LANES * (hh + 1)] for hh in range(2)]

    def step(j, carry, masked):
        new = []
        for hh in range(2):
            m, l, acc = carry[hh]
            k, v = _attn_tiles(q_ref, k_ref, v_ref, hh, j)
            s = lax.dot_general(qs[hh], k, (((1,), (1,)), ((), ())), preferred_element_type=F32)
            if masked:
                s = jnp.where(causal, s, -jnp.inf)
            m_new = jnp.maximum(m, jnp.max(s, axis=-1, keepdims=True))
            alpha = jnp.exp2(m - m_new)
            p = jnp.exp2(s - m_new)
            l_new = alpha * l + jnp.sum(p, axis=-1, keepdims=True)
            acc_new = alpha * acc + jnp.dot(p.astype(BF16), v, preferred_element_type=F32)
            new.append((m_new, l_new, acc_new))
        return tuple(new)

    one = (jnp.full((tq, 1), -jnp.inf, F32), jnp.zeros((tq, 1), F32), jnp.zeros((tq, LANES), F32))
    carry = lax.fori_loop(0, qi, functools.partial(step, masked=False), (one, one))
    (_, l0, a0), (_, l1, a1) = step(qi, carry, True)
    lane = lax.broadcasted_iota(jnp.int32, (tq, LANES), 1)
    o_ref[0] = jnp.where(lane < HEAD_DIM, a0 / l0, a1 / l1).astype(BF16)


def _first_key_tile(edge_ref, qi, group, nh):
    first = edge_ref[0, qi, 0:1, :]
    skipped = jnp.zeros((1, LANES), F32)
    for j in range(edge_ref.shape[1]):
        skipped += jnp.where(first - edge_ref[0, j, 1:2, :] <= UNDERFLOW_LOG2, 1.0, 0.0)
    lane = lax.broadcasted_iota(jnp.int32, (1, LANES), 1)
    in_group = jnp.logical_and(lane >= group * nh, lane < (group + 1) * nh)
    return jnp.min(jnp.where(in_group, skipped, float(edge_ref.shape[1]))).astype(jnp.int32)


def _attn_bounded_kernel(edge_ref, q_ref, k_ref, v_ref, o_ref, acc_ref):
    tq = q_ref.shape[1]
    nh = q_ref.shape[2] // LANES
    qi = pl.program_id(2)
    j0 = _first_key_tile(edge_ref, qi, pl.program_id(1), nh)
    causal = _causal_mask(tq)
    qs = [q_ref[0, :, LANES * hh:LANES * (hh + 1)] for hh in range(nh)]

    def tiles(js, masked_last):
        sums = [None] * nh
        for idx, j in enumerate(js):
            for hh in range(nh):
                k, v = _attn_tiles(q_ref, k_ref, v_ref, hh, j)
                s = lax.dot_general(qs[hh], k, (((1,), (1,)), ((), ())), preferred_element_type=F32)
                if masked_last and idx == len(js) - 1:
                    s = jnp.where(causal, s, -jnp.inf)
                pv = jnp.dot(jnp.exp2(s).astype(BF16), v, preferred_element_type=F32)
                sums[hh] = pv if sums[hh] is None else sums[hh] + pv
        return sums

    def trips(n, base):
        def body(jj, carry):
            for hh, part in enumerate(tiles([base + n * jj + u for u in range(n)], False)):
                acc_ref[hh] += part
            return carry
        return body

    def finish(js):
        sums = tiles(js, True)
        lane = lax.broadcasted_iota(jnp.int32, (tq, LANES), 1)
        for p in range(nh // 2):
            a0, a1 = acc_ref[2 * p] + sums[2 * p], acc_ref[2 * p + 1] + sums[2 * p + 1]
            o_ref[0, :, LANES * p:LANES * (p + 1)] = jnp.where(
                lane < HEAD_DIM, a0 / a0[:, HEAD_DIM:HEAD_DIM + 1], a1 / a1[:, 0:1]).astype(BF16)

    n = qi - j0
    acc_ref[...] = jnp.zeros_like(acc_ref)
    lax.fori_loop(0, n >> 2, trips(4, j0), 0)
    lax.fori_loop(0, (n >> 1) & 1, trips(2, j0 + ((n >> 2) << 2)), 0)

    @pl.when((n & 1) == 1)
    def _():
        finish([qi - 1, qi])

    @pl.when((n & 1) == 0)
    def _():
        finish([qi])


def _attention(q, k, v, bounded, edges, heads_per_step):
    bsz, s, _ = q.shape
    assert TQ == TK

    def call(body, name, nh, is_bounded):
        w = nh * LANES
        in_specs = [
            pl.BlockSpec((1, TQ, w), lambda b, g, i: (b, i, g)),
            pl.BlockSpec((1, s, w), lambda b, g, i: (b, 0, g)),
            pl.BlockSpec((1, s, w), lambda b, g, i: (b, 0, g)),
        ]
        scratch = []
        if is_bounded:
            in_specs.insert(0, pl.BlockSpec((1,) + edges.shape[1:], lambda b, g, i: (b, 0, 0, 0)))
            scratch = [pltpu.VMEM((nh, TQ, LANES), F32)]
        return pl.pallas_call(
            body,
            grid=(bsz, HEADS // nh, s // TQ),
            in_specs=in_specs,
            out_specs=pl.BlockSpec((1, TQ, w // 2), lambda b, g, i: (b, i, g)),
            out_shape=jax.ShapeDtypeStruct((bsz, s, PAIRS * LANES), BF16),
            scratch_shapes=scratch,
            compiler_params=_params("arbitrary", "arbitrary", "arbitrary"),
            name=name,
        )

    fast = call(_attn_bounded_kernel, "attn_bounded", heads_per_step, True)
    online = call(_attn_online_kernel, "attn_online", 2, False)
    return lax.cond(bounded, fast, lambda e, *qkv: online(*qkv), edges, q, k, v)


def _post_kernel(o_ref, x_ref, g1_ref, sh_ref, sc_ref, nrm_ref, wout_ref, wrhl_ref, br_ref,
                 x1_ref, h2_ref, comb_ref):
    tm = x_ref.shape[1]
    y = jnp.dot(o_ref[0], wout_ref[...], preferred_element_type=F32)
    x1 = x_ref[0] + g1_ref[0] * y
    x1_ref[0] = x1
    h2 = _modulated_norm(x1, nrm_ref[...], sc_ref[0], sh_ref[0])
    h_hi = h2.astype(BF16)
    h2_ref[0] = h_hi
    h_lo = (h2 - h_hi.astype(F32)).astype(BF16)
    hw = jnp.dot(h_hi, wrhl_ref[...], preferred_element_type=F32)
    lg = (hw[:, :LANES] + hw[:, LANES:]
          + jnp.dot(h_lo, wrhl_ref[:, :LANES], preferred_element_type=F32)) + br_ref[...]
    lane = lax.broadcasted_iota(jnp.int32, (tm, LANES), 1)
    lanef = lane.astype(F32)
    neg = -jnp.inf
    is_g = jnp.logical_and(lane >= N_EXPERTS, lane < N_EXPERTS + N_GROUPS)
    gl = jnp.where(is_g, lg, neg)
    gmax = jnp.max(gl, axis=-1, keepdims=True)
    gsum = jnp.sum(jnp.exp(gl - gmax), axis=-1, keepdims=True)
    g_top = 1.0 / gsum
    gidx = jnp.min(jnp.where(gl == gmax, lanef - N_EXPERTS, 1e9), axis=-1, keepdims=True)
    lane_grp = (lane >> 2).astype(F32)
    in_grp = jnp.logical_and(lane < N_EXPERTS, lane_grp == gidx)
    el = jnp.where(in_grp, lg, neg)
    e1 = jnp.max(el, axis=-1, keepdims=True)
    i1 = jnp.min(jnp.where(el == e1, lanef, 1e9), axis=-1, keepdims=True)
    el2 = jnp.where(lanef == i1, neg, el)
    e2 = jnp.max(el2, axis=-1, keepdims=True)
    i2 = jnp.min(jnp.where(el2 == e2, lanef, 1e9), axis=-1, keepdims=True)
    t = jnp.exp(e2 - e1)
    w1 = g_top / (1.0 + t)
    w2 = g_top * t / (1.0 + t)
    comb_ref[0] = jnp.where(lanef == i1, w1, jnp.where(lanef == i2, w2, 0.0))


def _post(o, x, g1, sh, sc, nrm, w_out, w_grp, b_grp, w_rt, b_rt):
    bsz, s, d = x.shape
    tm = TM_POST
    wr = jnp.zeros((d, LANES), F32).at[:, :N_EXPERTS].set(w_rt)
    wr = wr.at[:, N_EXPERTS:N_EXPERTS + N_GROUPS].set(w_grp)
    wrh = wr.astype(BF16)
    wrhl = jnp.concatenate([wrh, (wr - wrh.astype(F32)).astype(BF16)], axis=1)
    br =jnp.zeros((1, LANES), F32).at[0, :N_EXPERTS].set(b_rt)
    br = br.at[0, N_EXPERTS:N_EXPERTS + N_GROUPS].set(b_grp)
    full = lambda shape: pl.BlockSpec(shape, lambda b, t: (0,) * len(shape))
    vec = pl.BlockSpec((1, 1, d), lambda b, t: (b, 0, 0))
    tile = lambda w: pl.BlockSpec((1, tm, w), lambda b, t: (b, t, 0))
    return pl.pallas_call(
        _post_kernel,
        grid=(bsz, s // tm),
        in_specs=[tile(d), tile(d), vec, vec, vec, full((1, d)), full((d, d)),
                  full((d, 2 * LANES)), full((1, LANES))],
        out_specs=[tile(d), tile(d), tile(LANES)],
        out_shape=[
            jax.ShapeDtypeStruct((bsz, s, d), F32),
            jax.ShapeDtypeStruct((bsz, s, d), BF16),
            jax.ShapeDtypeStruct((bsz, s, LANES), F32),
        ],
        compiler_params=_params("arbitrary", "arbitrary"),
        name="post_router",
    )(o, x, g1, sh, sc, nrm.reshape(1, d), w_out.astype(BF16), wrhl, br)


def _moe_kernel(h_ref, comb_ref, x1_ref, g2_ref, wg_ref, wu_ref, wd_ref, out_ref, acc_ref):
    tm = h_ref.shape[0]
    n_block = wg_ref.shape[1]
    step = pl.program_id(1)
    last = N_EXPERTS // n_block - 1
    assert last >= 1
    h = h_ref[...]
    comb = comb_ref[...]
    lane = lax.broadcasted_iota(jnp.int32, (tm, LANES), 1)
    total = None
    for j in range(n_block):
        a = jnp.dot(h, wg_ref[0, j], preferred_element_type=F32)
        u = jnp.dot(h, wu_ref[0, j], preferred_element_type=F32)
        cw = jnp.sum(jnp.where(lane == step * n_block + j, comb, 0.0), axis=-1, keepdims=True)
        hid = (a * _sigmoid(a)) * u * cw
        part = jnp.dot(hid.astype(BF16), wd_ref[0, j], preferred_element_type=F32)
        total = part if total is None else total + part

    @pl.when(step == 0)
    def _():
        acc_ref[...] = total

    @pl.when(jnp.logical_and(step > 0, step < last))
    def _():
        acc_ref[...] += total

    @pl.when(step == last)
    def _():
        out_ref[...] = x1_ref[...] + g2_ref[0] * (acc_ref[...] + total)


def _moe(h2, comb, x1, g2, w_gate, w_up, w_down, layer):
    bsz, s, d = x1.shape
    t = bsz * s
    tm = TM_MOE
    per_batch = s // tm
    ne = MOE_EXPERTS_PER_STEP
    out = pl.pallas_call(
        _moe_kernel,
        grid=(t // tm, N_EXPERTS // MOE_EXPERTS_PER_STEP),
        in_specs=[
            pl.BlockSpec((tm, d), lambda i, e: (i, 0)),
            pl.BlockSpec((tm, LANES), lambda i, e: (i, 0)),
            pl.BlockSpec((tm, d), lambda i, e: (i, 0)),
            pl.BlockSpec((1, 1, d), lambda i, e: (i // per_batch, 0, 0)),
            pl.BlockSpec((1, ne, d, D_EXPERT), lambda i, e: (layer, e, 0, 0)),
            pl.BlockSpec((1, ne, d, D_EXPERT), lambda i, e: (layer, e, 0, 0)),
            pl.BlockSpec((1, ne, D_EXPERT, d), lambda i, e: (layer, e, 0, 0)),
        ],
        out_specs=pl.BlockSpec((tm, d), lambda i, e: (i, 0)),
        out_shape=jax.ShapeDtypeStruct((t, d), F32),
        scratch_shapes=[pltpu.VMEM((tm, d), F32)],
        compiler_params=_params("arbitrary", "arbitrary"),
        name="moe_dense",
    )(h2.reshape(t, d), comb.reshape(t, LANES), x1.reshape(t, d), g2, w_gate, w_up, w_down)
    return out.reshape(bsz, s, d)


def kernel(x, c, positions, w_ada, b_ada, mix_norm, ffn_norm, fox_w_in, fox_b_f, fox_q_norm, fox_k_norm,
           fox_w_out, mla_w_in, mla_q_lat_norm, mla_kv_lat_norm, mla_w_uq, mla_w_ukv, mla_q_norm, mla_k_norm,
           mla_w_out, moe_w_grp, moe_b_grp, moe_w_rt, moe_b_rt, moe_w_gate, moe_w_up, moe_w_down):
    depth = w_ada.shape[0]
    d = x.shape[-1]
    mod = _ada_modulation(c, w_ada, b_ada)
    wg, wu, wd = moe_w_gate.astype(BF16), moe_w_up.astype(BF16), moe_w_down.astype(BF16)
    for i in range(depth):
        sh1, sc1, g1, sh2, sc2, g2 = [mod[i, :, None, d * n:d * (n + 1)] for n in range(6)]
        j = i // 2
        if i % 2 == 0:
            bounded, b_hi, b_lo = _score_bound(fox_q_norm[j], fox_k_norm[j], HEAD_DIM)
            q, k, v, edges = _fox_pre(x, sh1, sc1, mix_norm[i], fox_w_in[j], fox_b_f[j], fox_q_norm[j],
                                      fox_k_norm[j], b_hi, b_lo)
            nh = FOX_HEADS_PER_STEP
            w_out = fox_w_out[j]
        else:
            bounded, b_hi, b_lo = _score_bound(mla_q_norm[j], mla_k_norm[j], MLA_QK)
            q, k, v = _mla_pre(x, positions, sh1, sc1, mix_norm[i], mla_w_in[j], mla_q_lat_norm[j],
                               mla_kv_lat_norm[j], mla_w_uq[j], mla_w_ukv[j], mla_q_norm[j], mla_k_norm[j],
                               b_hi, b_lo)
            nh = MLA_HEADS_PER_STEP
            edges = jnp.zeros((x.shape[0], x.shape[1] // TQ, 8, LANES), F32)
            w_out = mla_w_out[j]
        o = _attention(q, k, v, bounded, edges, nh)
        x1, h2, comb = _post(o, x, g1, sh2, sc2, ffn_norm[i], w_out, moe_w_grp[i], moe_b_grp[i],
                             moe_w_rt[i], moe_b_rt[i])
        x = _moe(h2, comb, x1, g2, wg, wu, wd, i)
    return x
```

```python
import functools
import math

import numpy as np
import jax
import jax.numpy as jnp
from jax import lax
from jax.experimental import pallas as pl
from jax.experimental.pallas import tpu as pltpu

F32 = jnp.float32
BF16 = jnp.bfloat16

D_MODEL = 1024
HEADS = 16
HEAD_DIM = 64
LANES = 128
PAIRS = HEADS // 2
EPS = 1e-6
MLA_Q_RANK = 384
MLA_KV_RANK = 256
MLA_ROPE = 32
MLA_QK = HEAD_DIM + MLA_ROPE
ROPE_THETA = 10000.0
N_GROUPS = 4
EPG = 4
N_EXPERTS = 16
D_EXPERT = 256

VMEM_LIMIT = 56 * 1024 * 1024
LOG2E = math.log2(math.e)
MAX_FAST_BOUND = 40.0
UNDERFLOW_LOG2 = -150.0
FOX_HEADS_PER_STEP = 2
MLA_HEADS_PER_STEP = 4

TM_PRE = 512
TM_POST = 512
TM_MOE = 1024
TQ = 512
TK = 512
assert TM_PRE == TQ == TK

CUM_HI, CUM_MID, CUM_LO, CUM_ONE = 0, 16, 32, 48


def _params(*sem):
    return pltpu.CompilerParams(dimension_semantics=sem, vmem_limit_bytes=VMEM_LIMIT)


def _sigmoid(x):
    return 1.0 / (1.0 + jnp.exp(-x))


def _modulated_norm(x, gain, scale, shift):
    ms = jnp.mean(x * x, axis=-1, keepdims=True)
    return (x * lax.rsqrt(ms + EPS) * gain) * (1.0 + scale) + shift


def _ada_kernel(c_ref, w_ref, b_ref, o_ref):
    w = w_ref[0]
    for b in range(c_ref.shape[0]):
        c = c_ref[b]
        act = c * _sigmoid(c)
        o_ref[0, b] = jnp.sum(act * w, axis=0, keepdims=True) + b_ref[0]


def _ada_modulation(c, w_ada, b_ada):
    depth, d, n = w_ada.shape
    bsz = c.shape[0]
    tn = 1536
    out = pl.pallas_call(
        _ada_kernel,
        grid=(depth, n // tn),
        in_specs=[
            pl.BlockSpec((bsz, d, 1), lambda i, j: (0, 0, 0)),
            pl.BlockSpec((1, d, tn), lambda i, j: (i, 0, j)),
            pl.BlockSpec((1, 1, tn), lambda i, j: (i, 0, j)),
        ],
        out_specs=pl.BlockSpec((1, bsz, 1, tn), lambda i, j: (i, 0, 0, j)),
        out_shape=jax.ShapeDtypeStruct((depth, bsz, 1, n), F32),
        compiler_params=_params("arbitrary", "arbitrary"),
        name="ada_mod",
    )(c.reshape(bsz, d, 1), w_ada, b_ada.reshape(depth, 1, n))
    return out.reshape(depth, bsz, n)


def _pair_rmsnorm(xp, gain, lo_half):
    sq = xp * xp
    s_all = jnp.sum(sq, axis=-1, keepdims=True)
    s_lo = jnp.sum(jnp.where(lo_half, sq, 0.0), axis=-1, keepdims=True)
    ms = jnp.where(lo_half, s_lo, s_all - s_lo) * (1.0 / HEAD_DIM)
    return xp * lax.rsqrt(ms + EPS) * gain


def _value_blocks(vp, lo_half, lane):
    even = jnp.where(lo_half, vp, jnp.where(lane == HEAD_DIM, 1.0, 0.0))
    odd = jnp.where(lo_half, jnp.where(lane == 0, 1.0, 0.0), vp)
    return even.astype(BF16), odd.astype(BF16)


def _fox_pre_kernel(x_ref, sh_ref, sc_ref, nrm_ref, wqkv_ref, wf_ref, bf_ref, gq_ref, gk_ref,
                    selq_ref, selk_ref, off_ref, q_ref, k_ref, v_ref, edge_ref, carry_ref):
    tm = x_ref.shape[1]

    @pl.when(pl.program_id(1) == 0)
    def _():
        carry_ref[...] = jnp.zeros_like(carry_ref)

    h = _modulated_norm(x_ref[0], nrm_ref[...], sc_ref[0], sh_ref[0])
    hb = h.astype(BF16)
    proj = jnp.dot(hb, wqkv_ref[...], preferred_element_type=F32)
    fl = jnp.dot(hb, wf_ref[...], preferred_element_type=F32) + bf_ref[...]
    logf = jnp.minimum(fl, 0.0) - jnp.log(1.0 + jnp.exp(-jnp.abs(fl)))

    row = lax.broadcasted_iota(jnp.int32, (tm, tm), 0)
    col = lax.broadcasted_iota(jnp.int32, (tm, tm), 1)
    tri = jnp.where(row >= col, 1.0, 0.0).astype(BF16)
    l_hi = logf.astype(BF16)
    l_r = logf - l_hi.astype(F32)
    l_mid = l_r.astype(BF16)
    l_lo = (l_r - l_mid.astype(F32)).astype(BF16)
    cum = (jnp.dot(tri, l_hi, preferred_element_type=F32) + jnp.dot(tri, l_mid, preferred_element_type=F32)
           + jnp.dot(tri, l_lo, preferred_element_type=F32)) + carry_ref[...]
    carry_ref[...] = cum[tm - 1:tm, :]

    lane = lax.broadcasted_iota(jnp.int32, (tm, LANES), 1)
    cum2 = cum * LOG2E
    edge_ref[0, 0, 0:1, :] = cum2[0:1, :]
    edge_ref[0, 0, 1:2, :] = cum2[tm - 1:tm, :]
    edge_ref[0, 0, 2:8, :] = jnp.zeros((6, LANES), F32)
    hi = cum2.astype(BF16).astype(F32)
    r1 = cum2 - hi
    mid = r1.astype(BF16).astype(F32)
    lo = r1 - mid
    parts = jnp.where(lane < CUM_MID, hi,
                      jnp.where(lane < CUM_LO, mid,
                                jnp.where(lane < CUM_ONE, lo, off_ref[...])))
    pb = parts.astype(BF16)
    augq = jnp.dot(pb, selq_ref[...], preferred_element_type=F32)
    augk = jnp.dot(pb, selk_ref[...], preferred_element_type=F32)

    lo_half = lane < HEAD_DIM
    d = D_MODEL
    for p in range(PAIRS):
        qn = _pair_rmsnorm(proj[:, LANES * p:LANES * (p + 1)], gq_ref[...], lo_half) * (HEAD_DIM ** -0.5 * LOG2E)
        kn = _pair_rmsnorm(proj[:, d + LANES * p:d + LANES * (p + 1)], gk_ref[...], lo_half)
        e0, e1, e2 = 2 * LANES * p, 2 * LANES * p + LANES, 2 * LANES * (p + 1)
        q_ref[0, :, e0:e1] = jnp.where(lo_half, qn, augq[:, e0:e1]).astype(BF16)
        q_ref[0, :, e1:e2] = jnp.where(lo_half, augq[:, e1:e2], qn).astype(BF16)
        k_ref[0, :, e0:e1] = jnp.where(lo_half, kn, augk[:, e0:e1]).astype(BF16)
        k_ref[0, :, e1:e2] = jnp.where(lo_half, augk[:, e1:e2], kn).astype(BF16)
        v_even, v_odd = _value_blocks(proj[:, 2 * d + LANES * p:2 * d + LANES * (p + 1)], lo_half, lane)
        v_ref[0, :, e0:e1] = v_even
        v_ref[0, :, e1:e2] = v_odd


def _fox_selectors():
    selq = np.zeros((LANES, HEADS * LANES), np.float32)
    selk = np.zeros((LANES, HEADS * LANES), np.float32)
    for h in range(HEADS):
        base = h * LANES + (HEAD_DIM if h % 2 == 0 else 0)
        for j, src in enumerate((CUM_HI, CUM_MID, CUM_LO)):
            selq[src + h, base + j] = 1.0
            selq[CUM_ONE, base + 3 + j] = 1.0
            selk[CUM_ONE, base + j] = 1.0
            selk[src + h, base + 3 + j] = -1.0
        for j in range(2):
            selq[CUM_ONE + 1 + j, base + 6 + j] = 1.0
            selk[CUM_ONE, base + 6 + j] = 1.0
    return jnp.asarray(selq, BF16), jnp.asarray(selk, BF16)


def _score_bound(q_gain, k_gain, dim):
    bound = (jnp.max(jnp.abs(q_gain)) * jnp.max(jnp.abs(k_gain)) * (LOG2E * math.sqrt(dim))).astype(F32)
    usable = bound <= MAX_FAST_BOUND
    bound = jnp.where(usable, bound, 0.0)
    hi = bound.astype(BF16).astype(F32)
    return usable, hi, bound - hi


def _fox_pre(x, sh, sc, nrm, w_in, b_f, q_norm, k_norm, bound_hi, bound_lo):
    bsz, s, d = x.shape
    tm = TM_PRE
    wqkv = w_in.astype(BF16)
    wf = w_in[:, 3 * d:]
    zpad = jnp.zeros((d, LANES - 3 * HEADS), F32)
    wf3 = jnp.concatenate([wf, wf, wf, zpad], axis=1).astype(BF16)
    bf3 = jnp.concatenate([b_f, b_f, b_f, jnp.zeros((LANES - 3 * HEADS,), F32)]).reshape(1, LANES)
    gq = jnp.tile(q_norm, 2).reshape(1, LANES)
    gk = jnp.tile(k_norm, 2).reshape(1, LANES)
    selq, selk = _fox_selectors()
    off = jnp.zeros((1, LANES), F32).at[0, CUM_ONE].set(1.0)
    off = off.at[0, CUM_ONE + 1].set(-bound_hi).at[0, CUM_ONE + 2].set(-bound_lo)
    full = lambda shape: pl.BlockSpec(shape, lambda b, t: (0,) * len(shape))
    vec = pl.BlockSpec((1, 1, d), lambda b, t: (b, 0, 0))
    hq = HEADS * LANES
    head_blocks = pl.BlockSpec((1, tm, hq), lambda b, t: (b, t, 0))
    return pl.pallas_call(
        _fox_pre_kernel,
        grid=(bsz, s // tm),
        in_specs=[
            pl.BlockSpec((1, tm, d), lambda b, t: (b, t, 0)),
            vec, vec, full((1, d)),
            full((d, 3 * d)), full((d, LANES)), full((1, LANES)), full((1, LANES)), full((1, LANES)),
            full((LANES, hq)), full((LANES, hq)), full((1, LANES)),
        ],
        out_specs=[head_blocks] * 3 + [pl.BlockSpec((1, 1, 8, LANES), lambda b, t: (b, t, 0, 0))],
        out_shape=[jax.ShapeDtypeStruct((bsz, s, hq), BF16)] * 3
        + [jax.ShapeDtypeStruct((bsz, s // tm, 8, LANES), F32)],
        scratch_shapes=[pltpu.VMEM((1, LANES), F32)],
        compiler_params=_params("arbitrary", "arbitrary"),
        name="fox_pre",
    )(x, sh, sc, nrm.reshape(1, d), wqkv, wf3, bf3, gq, gk, selq, selk, off)


def _mla_pre_kernel(x_ref, pos_ref, sh_ref, sc_ref, nrm_ref, win_ref, gql_ref, gkvl_ref,
                    wuq_ref, wuqs_ref, wkn_ref, wv_ref, gq_ref, gqs_ref, gk_ref, invf_ref, qoff_ref, koff_ref,
                    q_ref, k_ref, v_ref):
    tm = x_ref.shape[1]
    h = _modulated_norm(x_ref[0], nrm_ref[...], sc_ref[0], sh_ref[0])
    proj = jnp.dot(h.astype(BF16), win_ref[...], preferred_element_type=F32)
    cq = proj[:, :MLA_Q_RANK]
    ckv = proj[:, MLA_Q_RANK:MLA_Q_RANK + MLA_KV_RANK]
    kr = proj[:, MLA_Q_RANK + MLA_KV_RANK:]
    cqn = cq * lax.rsqrt(jnp.mean(cq * cq, axis=-1, keepdims=True) + EPS) * gql_ref[...]
    ckvn = ckv * lax.rsqrt(jnp.mean(ckv * ckv, axis=-1, keepdims=True) + EPS) * gkvl_ref[...]
    cqb = cqn.astype(BF16)
    ckvb = ckvn.astype(BF16)
    qall = jnp.dot(cqb, wuq_ref[...], preferred_element_type=F32)
    qswap = jnp.dot(cqb, wuqs_ref[...], preferred_element_type=F32)
    knall = jnp.dot(ckvb, wkn_ref[...], preferred_element_type=F32)
    vall = jnp.dot(ckvb, wv_ref[...], preferred_element_type=F32)

    lane = lax.broadcasted_iota(jnp.int32, (tm, LANES), 1)
    lo_half = lane < HEAD_DIM
    for p in range(PAIRS):
        v_even, v_odd = _value_blocks(vall[:, LANES * p:LANES * (p + 1)], lo_half, lane)
        v_ref[0, :, 2 * LANES * p:2 * LANES * p + LANES] = v_even
        v_ref[0, :, 2 * LANES * p + LANES:2 * LANES * (p + 1)] = v_odd
    first_half = lane < HEAD_DIM + MLA_ROPE // 2
    ang = pos_ref[0] * invf_ref[...]
    cosf = jnp.cos(ang)
    sinf = jnp.sin(ang)
    sins = jnp.where(first_half, -sinf, sinf)
    half = MLA_ROPE // 2

    def swap_halves(t):
        return jnp.where(first_half, pltpu.roll(t, LANES - half, axis=1), pltpu.roll(t, half, axis=1))

    gk = gk_ref[...]
    q_cos = gq_ref[...] * cosf
    q_sin = gqs_ref[...] * sins
    k_cos = gk * cosf
    sw_k = swap_halves(kr * gk) * sins
    inv_dim = 1.0 / MLA_QK
    for hd in range(HEADS):
        sl = slice(LANES * hd, LANES * (hd + 1))
        qh = qall[:, sl]
        rq = lax.rsqrt(jnp.sum(qh * qh, axis=-1, keepdims=True) * inv_dim + EPS)
        qf = (qh * q_cos + qswap[:, sl] * q_sin) * rq
        q_ref[0, :, sl] = (qf + qoff_ref[...]).astype(BF16)
        kh = knall[:, sl] + kr
        rk = lax.rsqrt(jnp.sum(kh * kh, axis=-1, keepdims=True) * inv_dim + EPS)
        kf = rk * (kh * k_cos + sw_k)
        k_ref[0, :, sl] = (kf + koff_ref[...]).astype(BF16)


def _mla_pre(x, positions, sh, sc, nrm, w_in, q_lat_norm, kv_lat_norm, w_uq, w_ukv, q_norm, k_norm,
             bound_hi, bound_lo):
    bsz, s, d = x.shape
    tm = TM_PRE
    hq = HEADS * LANES
    lat = MLA_Q_RANK + MLA_KV_RANK
    kr_cols = jnp.zeros((d, LANES), F32).at[:, HEAD_DIM:HEAD_DIM + MLA_ROPE].set(w_in[:, lat:])
    win = jnp.concatenate([w_in[:, :lat], kr_cols], axis=1).astype(BF16)
    half = MLA_ROPE // 2
    r0, r1, r2 = HEAD_DIM, HEAD_DIM + half, HEAD_DIM + MLA_ROPE

    def swap_rotary(t):
        out = jnp.zeros(t.shape[:-1] + (LANES,), F32)
        return out.at[..., r0:r1].set(t[..., r1:r2]).at[..., r1:r2].set(t[..., r0:r1])

    uq = w_uq.reshape(MLA_Q_RANK, HEADS, MLA_QK)
    wuq = jnp.zeros((MLA_Q_RANK, HEADS, LANES), F32).at[:, :, :MLA_QK].set(uq).reshape(
        MLA_Q_RANK, hq).astype(BF16)
    wuqs = swap_rotary(uq).reshape(MLA_Q_RANK, hq).astype(BF16)
    ukv = w_ukv.reshape(MLA_KV_RANK, HEADS, 2 * HEAD_DIM)
    wkn = jnp.zeros((MLA_KV_RANK, HEADS, LANES), F32).at[:, :, :HEAD_DIM].set(
        ukv[:, :, :HEAD_DIM]).reshape(MLA_KV_RANK, hq).astype(BF16)
    wv = ukv[:, :, HEAD_DIM:].reshape(MLA_KV_RANK, HEADS * HEAD_DIM).astype(BF16)
    pad = jnp.zeros((LANES - MLA_QK,), F32)
    gq = jnp.concatenate([q_norm, pad]).reshape(1, LANES) * (MLA_QK ** -0.5 * LOG2E)
    gqs = swap_rotary(gq)
    gk = jnp.concatenate([k_norm, pad]).reshape(1, LANES)
    inv_freq = ROPE_THETA ** (-jnp.arange(0, half, dtype=F32) / half)
    invf = jnp.zeros((LANES,), F32).at[HEAD_DIM:HEAD_DIM + half].set(inv_freq)
    invf = invf.at[HEAD_DIM + half:HEAD_DIM + MLA_ROPE].set(inv_freq).reshape(1, LANES)
    pos = positions.astype(F32).reshape(bsz, s, 1)
    qoff = jnp.zeros((1, LANES), F32).at[0, MLA_QK].set(-bound_hi).at[0, MLA_QK + 1].set(-bound_lo)
    koff = jnp.zeros((1, LANES), F32).at[0, MLA_QK:MLA_QK + 2].set(1.0)
    full = lambda shape: pl.BlockSpec(shape, lambda b, t: (0,) * len(shape))
    vec = pl.BlockSpec((1, 1, d), lambda b, t: (b, 0, 0))
    head_blocks = pl.BlockSpec((1, tm, hq), lambda b, t: (b, t, 0))
    return pl.pallas_call(
        _mla_pre_kernel,
        grid=(bsz, s // tm),
        in_specs=[
            pl.BlockSpec((1, tm, d), lambda b, t: (b, t, 0)),
            pl.BlockSpec((1, tm, 1), lambda b, t: (b, t, 0)),
            vec, vec, full((1, d)),
            full((d, lat + LANES)), full((1, MLA_Q_RANK)), full((1, MLA_KV_RANK)),
            full((MLA_Q_RANK, hq)), full((MLA_Q_RANK, hq)), full((MLA_KV_RANK, hq)), full((MLA_KV_RANK, d)),
            full((1, LANES)), full((1, LANES)), full((1, LANES)), full((1, LANES)), full((1, LANES)),
            full((1, LANES)),
        ],
        out_specs=[head_blocks] * 3,
        out_shape=[jax.ShapeDtypeStruct((bsz, s, hq), BF16)] * 3,
        compiler_params=_params("arbitrary", "arbitrary"),
        name="mla_pre",
    )(x, pos, sh, sc, nrm.reshape(1, d), win, q_lat_norm.reshape(1, -1), kv_lat_norm.reshape(1, -1),
      wuq, wuqs, wkn, wv, gq, gqs, gk, invf, qoff, koff)


def _attn_tiles(q_ref, k_ref, v_ref, hh, j):
    hs = slice(LANES * hh, LANES * (hh + 1))
    start = pl.multiple_of(j * TK, TK)
    return k_ref[0, pl.ds(start, TK), hs], v_ref[0, pl.ds(start, TK), hs]


def _causal_mask(tq):
    row = lax.broadcasted_iota(jnp.int32, (tq, TK), 0)
    col = lax.broadcasted_iota(jnp.int32, (tq, TK), 1)
    return row >= col


def _attn_online_kernel(q_ref, k_ref, v_ref, o_ref):
    tq = q_ref.shape[1]
    qi = pl.program_id(2)
    causal = _causal_mask(tq)
    qs = [q_ref[0, :, LANES * hh:LANES * (hh + 1)] for hh in range(2)]

    def step(j, carry, masked):
        new = []
        for hh in range(2):
            m, l, acc = carry[hh]
            k, v = _attn_tiles(q_ref, k_ref, v_ref, hh, j)
            s = lax.dot_general(qs[hh], k, (((1,), (1,)), ((), ())), preferred_element_type=F32)
            if masked:
                s = jnp.where(causal, s, -jnp.inf)
            m_new = jnp.maximum(m, jnp.max(s, axis=-1, keepdims=True))
            alpha = jnp.exp2(m - m_new)
            p = jnp.exp2(s - m_new)
            l_new = alpha * l + jnp.sum(p, axis=-1, keepdims=True)
            acc_new = alpha * acc + jnp.dot(p.astype(BF16), v, preferred_element_type=F32)
            new.append((m_new, l_new, acc_new))
        return tuple(new)

    one = (jnp.full((tq, 1), -jnp.inf, F32), jnp.zeros((tq, 1), F32), jnp.zeros((tq, LANES), F32))
    carry = lax.fori_loop(0, qi, functools.partial(step, masked=False), (one, one))
    (_, l0, a0), (_, l1, a1) = step(qi, carry, True)
    lane = lax.broadcasted_iota(jnp.int32, (tq, LANES), 1)
    o_ref[0] = jnp.where(lane < HEAD_DIM, a0 / l0, a1 / l1).astype(BF16)


def _first_key_tile(edge_ref, qi, group, nh):
    first = edge_ref[0, qi, 0:1, :]
    skipped = jnp.zeros((1, LANES), F32)
    for j in range(edge_ref.shape[1]):
        skipped += jnp.where(first - edge_ref[0, j, 1:2, :] <= UNDERFLOW_LOG2, 1.0, 0.0)
    lane = lax.broadcasted_iota(jnp.int32, (1, LANES), 1)
    in_group = jnp.logical_and(lane >= group * nh, lane < (group + 1) * nh)
    return jnp.min(jnp.where(in_group, skipped, float(edge_ref.shape[1]))).astype(jnp.int32)


def _attn_bounded_kernel(edge_ref, q_ref, k_ref, v_ref, o_ref, acc_ref):
    tq = q_ref.shape[1]
    nh = q_ref.shape[2] // LANES
    qi = pl.program_id(2)
    j0 = _first_key_tile(edge_ref, qi, pl.program_id(1), nh)
    causal = _causal_mask(tq)
    qs = [q_ref[0, :, LANES * hh:LANES * (hh + 1)] for hh in range(nh)]

    def tiles(js, masked_last):
        sums = [None] * nh
        for idx, j in enumerate(js):
            for hh in range(nh):
                k, v = _attn_tiles(q_ref, k_ref, v_ref, hh, j)
                s = lax.dot_general(qs[hh], k, (((1,), (1,)), ((), ())), preferred_element_type=F32)
                if masked_last and idx == len(js) - 1:
                    s = jnp.where(causal, s, -jnp.inf)
                pv = jnp.dot(jnp.exp2(s).astype(BF16), v, preferred_element_type=F32)
                sums[hh] = pv if sums[hh] is None else sums[hh] + pv
        return sums

    def trips(n, base):
        def body(jj, carry):
            for hh, part in enumerate(tiles([base + n * jj + u for u in range(n)], False)):
                acc_ref[hh] += part
            return carry
        return body

    def finish(js):
        sums = tiles(js, True)
        lane = lax.broadcasted_iota(jnp.int32, (tq, LANES), 1)
        for p in range(nh // 2):
            a0, a1 = acc_ref[2 * p] + sums[2 * p], acc_ref[2 * p + 1] + sums[2 * p + 1]
            o_ref[0, :, LANES * p:LANES * (p + 1)] = jnp.where(
                lane < HEAD_DIM, a0 / a0[:, HEAD_DIM:HEAD_DIM + 1], a1 / a1[:, 0:1]).astype(BF16)

    n = qi - j0
    acc_ref[...] = jnp.zeros_like(acc_ref)
    lax.fori_loop(0, n >> 2, trips(4, j0), 0)
    for left in range(4):
        @pl.when((n & 3) == left)
        def _(left=left):
            finish([qi - left + u for u in range(left + 1)])


def _attention(q, k, v, bounded, edges, heads_per_step):
    bsz, s, _ = q.shape
    assert TQ == TK

    def call(body, name, nh, is_bounded):
        w = nh * LANES
        in_specs = [
            pl.BlockSpec((1, TQ, w), lambda b, g, i: (b, i, g)),
            pl.BlockSpec((1, s, w), lambda b, g, i: (b, 0, g)),
            pl.BlockSpec((1, s, w), lambda b, g, i: (b, 0, g)),
        ]
        scratch = []
        if is_bounded:
            in_specs.insert(0, pl.BlockSpec((1,) + edges.shape[1:], lambda b, g, i: (b, 0, 0, 0)))
            scratch = [pltpu.VMEM((nh, TQ, LANES), F32)]
        return pl.pallas_call(
            body,
            grid=(bsz, HEADS // nh, s // TQ),
            in_specs=in_specs,
            out_specs=pl.BlockSpec((1, TQ, w // 2), lambda b, g, i: (b, i, g)),
            out_shape=jax.ShapeDtypeStruct((bsz, s, PAIRS * LANES), BF16),
            scratch_shapes=scratch,
            compiler_params=_params("arbitrary", "arbitrary", "arbitrary"),
            name=name,
        )

    fast = call(_attn_bounded_kernel, "attn_bounded", heads_per_step, True)
    online = call(_attn_online_kernel, "attn_online", 2, False)
    return lax.cond(bounded, fast, lambda e, *qkv: online(*qkv), edges, q, k, v)


def _post_kernel(o_ref, x_ref, g1_ref, sh_ref, sc_ref, nrm_ref, wout_ref, wrhl_ref, br_ref,
                 x1_ref, h2_ref, comb_ref):
    tm = x_ref.shape[1]
    y = jnp.dot(o_ref[0], wout_ref[...], preferred_element_type=F32)
    x1 = x_ref[0] + g1_ref[0] * y
    x1_ref[0] = x1
    h2 = _modulated_norm(x1, nrm_ref[...], sc_ref[0], sh_ref[0])
    h_hi = h2.astype(BF16)
    h2_ref[0] = h_hi
    h_lo = (h2 - h_hi.astype(F32)).astype(BF16)
    hw = jnp.dot(h_hi, wrhl_ref[...], preferred_element_type=F32)
    lg = (hw[:, :LANES] + hw[:, LANES:]
          + jnp.dot(h_lo, wrhl_ref[:, :LANES], preferred_element_type=F32)) + br_ref[...]
    lane = lax.broadcasted_iota(jnp.int32, (tm, LANES), 1)
    lanef = lane.astype(F32)
    neg = -jnp.inf
    is_g = jnp.logical_and(lane >= N_EXPERTS, lane < N_EXPERTS + N_GROUPS)
    gl = jnp.where(is_g, lg, neg)
    gmax = jnp.max(gl, axis=-1, keepdims=True)
    gsum = jnp.sum(jnp.exp(gl - gmax), axis=-1, keepdims=True)
    g_top = 1.0 / gsum
    gidx = jnp.min(jnp.where(gl == gmax, lanef - N_EXPERTS, 1e9), axis=-1, keepdims=True)
    lane_grp = (lane >> 2).astype(F32)
    in_grp = jnp.logical_and(lane < N_EXPERTS, lane_grp == gidx)
    el = jnp.where(in_grp, lg, neg)
    e1 = jnp.max(el, axis=-1, keepdims=True)
    i1 = jnp.min(jnp.where(el == e1, lanef, 1e9), axis=-1, keepdims=True)
    el2 = jnp.where(lanef == i1, neg, el)
    e2 = jnp.max(el2, axis=-1, keepdims=True)
    i2 = jnp.min(jnp.where(el2 == e2, lanef, 1e9), axis=-1, keepdims=True)
    t = jnp.exp(e2 - e1)
    w1 = g_top / (1.0 + t)
    w2 = g_top * t / (1.0 + t)
    comb_ref[0] = jnp.where(lanef == i1, w1, jnp.where(lanef == i2, w2, 0.0))


def _post(o, x, g1, sh, sc, nrm, w_out, w_grp, b_grp, w_rt, b_rt):
    bsz, s, d = x.shape
    tm = TM_POST
    wr = jnp.zeros((d, LANES), F32).at[:, :N_EXPERTS].set(w_rt)
    wr = wr.at[:, N_EXPERTS:N_EXPERTS + N_GROUPS].set(w_grp)
    wrh = wr.astype(BF16)
    wrhl = jnp.concatenate([wrh, (wr - wrh.astype(F32)).astype(BF16)], axis=1)
    br =jnp.zeros((1, LANES), F32).at[0, :N_EXPERTS].set(b_rt)
    br = br.at[0, N_EXPERTS:N_EXPERTS + N_GROUPS].set(b_grp)
    full = lambda shape: pl.BlockSpec(shape, lambda b, t: (0,) * len(shape))
    vec = pl.BlockSpec((1, 1, d), lambda b, t: (b, 0, 0))
    tile = lambda w: pl.BlockSpec((1, tm, w), lambda b, t: (b, t, 0))
    return pl.pallas_call(
        _post_kernel,
        grid=(bsz, s // tm),
        in_specs=[tile(d), tile(d), vec, vec, vec, full((1, d)), full((d, d)),
                  full((d, 2 * LANES)), full((1, LANES))],
        out_specs=[tile(d), tile(d), tile(LANES)],
        out_shape=[
            jax.ShapeDtypeStruct((bsz, s, d), F32),
            jax.ShapeDtypeStruct((bsz, s, d), BF16),
            jax.ShapeDtypeStruct((bsz, s, LANES), F32),
        ],
        compiler_params=_params("arbitrary", "arbitrary"),
        name="post_router",
    )(o, x, g1, sh, sc, nrm.reshape(1, d), w_out.astype(BF16), wrhl, br)


def _moe_kernel(h_ref, comb_ref, x1_ref, g2_ref, wg_ref, wu_ref, wd_ref, out_ref):
    tm = h_ref.shape[0]
    h = h_ref[...]
    comb = comb_ref[...]
    lane = lax.broadcasted_iota(jnp.int32, (tm, LANES), 1)
    total = None
    for e in range(N_EXPERTS):
        a = jnp.dot(h, wg_ref[0, e], preferred_element_type=F32)
        u = jnp.dot(h, wu_ref[0, e], preferred_element_type=F32)
        cw = jnp.sum(jnp.where(lane == e, comb, 0.0), axis=-1, keepdims=True)
        hid = (a * _sigmoid(a)) * u * cw
        part = jnp.dot(hid.astype(BF16), wd_ref[0, e], preferred_element_type=F32)
        total = part if total is None else total + part
    out_ref[...] = x1_ref[...] + g2_ref[0] * total


def _moe(h2, comb, x1, g2, w_gate, w_up, w_down, layer):
    bsz, s, d = x1.shape
    t = bsz * s
    tm = TM_MOE
    per_batch = s // tm
    resident = lambda shape: pl.BlockSpec((1, N_EXPERTS) + shape, lambda i: (layer, 0, 0, 0),
                                          pipeline_mode=pl.Buffered(1))
    out = pl.pallas_call(
        _moe_kernel,
        grid=(t // tm,),
        in_specs=[
            pl.BlockSpec((tm, d), lambda i: (i, 0)),
            pl.BlockSpec((tm, LANES), lambda i: (i, 0)),
            pl.BlockSpec((tm, d), lambda i: (i, 0)),
            pl.BlockSpec((1, 1, d), lambda i: (i // per_batch, 0, 0)),
            resident((d, D_EXPERT)), resident((d, D_EXPERT)), resident((D_EXPERT, d)),
        ],
        out_specs=pl.BlockSpec((tm, d), lambda i: (i, 0)),
        out_shape=jax.ShapeDtypeStruct((t, d), F32),
        compiler_params=_params("arbitrary"),
        name="moe_dense",
    )(h2.reshape(t, d), comb.reshape(t, LANES), x1.reshape(t, d), g2, w_gate, w_up, w_down)
    return out.reshape(bsz, s, d)


def kernel(x, c, positions, w_ada, b_ada, mix_norm, ffn_norm, fox_w_in, fox_b_f, fox_q_norm, fox_k_norm,
           fox_w_out, mla_w_in, mla_q_lat_norm, mla_kv_lat_norm, mla_w_uq, mla_w_ukv, mla_q_norm, mla_k_norm,
           mla_w_out, moe_w_grp, moe_b_grp, moe_w_rt, moe_b_rt, moe_w_gate, moe_w_up, moe_w_down):
    depth = w_ada.shape[0]
    d = x.shape[-1]
    mod = _ada_modulation(c, w_ada, b_ada)
    wg, wu, wd = moe_w_gate.astype(BF16), moe_w_up.astype(BF16), moe_w_down.astype(BF16)
    for i in range(depth):
        sh1, sc1, g1, sh2, sc2, g2 = [mod[i, :, None, d * n:d * (n + 1)] for n in range(6)]
        j = i // 2
        if i % 2 == 0:
            bounded, b_hi, b_lo = _score_bound(fox_q_norm[j], fox_k_norm[j], HEAD_DIM)
            q, k, v, edges = _fox_pre(x, sh1, sc1, mix_norm[i], fox_w_in[j], fox_b_f[j], fox_q_norm[j],
                                      fox_k_norm[j], b_hi, b_lo)
            nh = FOX_HEADS_PER_STEP
            w_out = fox_w_out[j]
        else:
            bounded, b_hi, b_lo = _score_bound(mla_q_norm[j], mla_k_norm[j], MLA_QK)
            q, k, v = _mla_pre(x, positions, sh1, sc1, mix_norm[i], mla_w_in[j], mla_q_lat_norm[j],
                               mla_kv_lat_norm[j], mla_w_uq[j], mla_w_ukv[j], mla_q_norm[j], mla_k_norm[j],
                               b_hi, b_lo)
            nh = MLA_HEADS_PER_STEP
            edges = jnp.zeros((x.shape[0], x.shape[1] // TQ, 8, LANES), F32)
            w_out = mla_w_out[j]
        o = _attention(q, k, v, bounded, edges, nh)
        x1, h2, comb = _post(o, x, g1, sh2, sc2, ffn_norm[i], w_out, moe_w_grp[i], moe_b_grp[i],
                             moe_w_rt[i], moe_b_rt[i])
        x = _moe(h2, comb, x1, g2, wg, wu, wd, i)
    return x
```

```python
import functools
import math

import numpy as np
import jax
import jax.numpy as jnp
from jax import lax
from jax.experimental import pallas as pl
from jax.experimental.pallas import tpu as pltpu

F32 = jnp.float32
BF16 = jnp.bfloat16

D_MODEL = 1024
HEADS = 16
HEAD_DIM = 64
LANES = 128
PAIRS = HEADS // 2
EPS = 1e-6
MLA_Q_RANK = 384
MLA_KV_RANK = 256
MLA_ROPE = 32
MLA_QK = HEAD_DIM + MLA_ROPE
ROPE_THETA = 10000.0
N_GROUPS = 4
EPG = 4
N_EXPERTS = 16
D_EXPERT = 256

VMEM_LIMIT = 56 * 1024 * 1024
LOG2E = math.log2(math.e)
MAX_FAST_BOUND = 40.0
UNDERFLOW_LOG2 = -150.0
FOX_HEADS_PER_STEP = 2
MLA_HEADS_PER_STEP = 4

TM_PRE = 512
TM_POST = 512
TM_MOE = 1024
TQ = 512
TK = 512
assert TM_PRE == TQ == TK

CUM_HI, CUM_MID, CUM_LO, CUM_ONE = 0, 16, 32, 48


def _params(*sem):
    return pltpu.CompilerParams(dimension_semantics=sem, vmem_limit_bytes=VMEM_LIMIT)


def _sigmoid(x):
    return 1.0 / (1.0 + jnp.exp(-x))


MOD_SHIFT1, MOD_SCALE1, MOD_GATE1, MOD_SHIFT2, MOD_SCALE2, MOD_GATE2 = range(6)


def _mod_spec(layer, which, bsz, batch_of):
    return pl.BlockSpec((1, 1, D_MODEL), lambda *g: ((layer * bsz + batch_of(*g)) * 6 + which, 0, 0))


def _modulated_norm(x, gain, scale, shift):
    ms = jnp.mean(x * x, axis=-1, keepdims=True)
    return (x * lax.rsqrt(ms + EPS) * gain) * (1.0 + scale) + shift


def _ada_kernel(c_ref, w_ref, b_ref, o_ref):
    w = w_ref[0]
    for b in range(c_ref.shape[0]):
        c = c_ref[b]
        act = c * _sigmoid(c)
        o_ref[0, b] = jnp.sum(act * w, axis=0, keepdims=True) + b_ref[0]


def _ada_modulation(c, w_ada, b_ada):
    depth, d, n = w_ada.shape
    bsz = c.shape[0]
    tn = 1536
    out = pl.pallas_call(
        _ada_kernel,
        grid=(depth, n // tn),
        in_specs=[
            pl.BlockSpec((bsz, d, 1), lambda i, j: (0, 0, 0)),
            pl.BlockSpec((1, d, tn), lambda i, j: (i, 0, j)),
            pl.BlockSpec((1, 1, tn), lambda i, j: (i, 0, j)),
        ],
        out_specs=pl.BlockSpec((1, bsz, 1, tn), lambda i, j: (i, 0, 0, j)),
        out_shape=jax.ShapeDtypeStruct((depth, bsz, 1, n), F32),
        compiler_params=_params("arbitrary", "arbitrary"),
        name="ada_mod",
    )(c.reshape(bsz, d, 1), w_ada, b_ada.reshape(depth, 1, n))
    return out.reshape(depth, bsz, n)


def _pair_rmsnorm(xp, gain, lo_half):
    sq = xp * xp
    s_all = jnp.sum(sq, axis=-1, keepdims=True)
    s_lo = jnp.sum(jnp.where(lo_half, sq, 0.0), axis=-1, keepdims=True)
    ms = jnp.where(lo_half, s_lo, s_all - s_lo) * (1.0 / HEAD_DIM)
    return xp * lax.rsqrt(ms + EPS) * gain


def _value_blocks(vp, lo_half, lane):
    even = jnp.where(lo_half, vp, jnp.where(lane == HEAD_DIM, 1.0, 0.0))
    odd = jnp.where(lo_half, jnp.where(lane == 0, 1.0, 0.0), vp)
    return even.astype(BF16), odd.astype(BF16)


def _fox_pre_kernel(x_ref, sh_ref, sc_ref, nrm_ref, wqkv_ref, wf_ref, bf_ref, gq_ref, gk_ref,
                    selq_ref, selk_ref, off_ref, q_ref, k_ref, v_ref, edge_ref, carry_ref):
    tm = x_ref.shape[1]

    @pl.when(pl.program_id(1) == 0)
    def _():
        carry_ref[...] = jnp.zeros_like(carry_ref)

    h = _modulated_norm(x_ref[0], nrm_ref[...], sc_ref[0], sh_ref[0])
    hb = h.astype(BF16)
    proj = jnp.dot(hb, wqkv_ref[...], preferred_element_type=F32)
    fl = jnp.dot(hb, wf_ref[...], preferred_element_type=F32) + bf_ref[...]
    logf = jnp.minimum(fl, 0.0) - jnp.log(1.0 + jnp.exp(-jnp.abs(fl)))

    row = lax.broadcasted_iota(jnp.int32, (tm, tm), 0)
    col = lax.broadcasted_iota(jnp.int32, (tm, tm), 1)
    tri = jnp.where(row >= col, 1.0, 0.0).astype(BF16)
    l_hi = logf.astype(BF16)
    l_r = logf - l_hi.astype(F32)
    l_mid = l_r.astype(BF16)
    l_lo = (l_r - l_mid.astype(F32)).astype(BF16)
    cum = (jnp.dot(tri, l_hi, preferred_element_type=F32) + jnp.dot(tri, l_mid, preferred_element_type=F32)
           + jnp.dot(tri, l_lo, preferred_element_type=F32)) + carry_ref[...]
    carry_ref[...] = cum[tm - 1:tm, :]

    lane = lax.broadcasted_iota(jnp.int32, (tm, LANES), 1)
    cum2 = cum * LOG2E
    edge_ref[0, 0, 0:1, :] = cum2[0:1, :]
    edge_ref[0, 0, 1:2, :] = cum2[tm - 1:tm, :]
    edge_ref[0, 0, 2:8, :] = jnp.zeros((6, LANES), F32)
    hi = cum2.astype(BF16).astype(F32)
    r1 = cum2 - hi
    mid = r1.astype(BF16).astype(F32)
    lo = r1 - mid
    parts = jnp.where(lane < CUM_MID, hi,
                      jnp.where(lane < CUM_LO, mid,
                                jnp.where(lane < CUM_ONE, lo, off_ref[...])))
    pb = parts.astype(BF16)
    augq = jnp.dot(pb, selq_ref[...], preferred_element_type=F32)
    augk = jnp.dot(pb, selk_ref[...], preferred_element_type=F32)

    lo_half = lane < HEAD_DIM
    d = D_MODEL
    for p in range(PAIRS):
        qn = _pair_rmsnorm(proj[:, LANES * p:LANES * (p + 1)], gq_ref[...], lo_half) * (HEAD_DIM ** -0.5 * LOG2E)
        kn = _pair_rmsnorm(proj[:, d + LANES * p:d + LANES * (p + 1)], gk_ref[...], lo_half)
        e0, e1, e2 = 2 * LANES * p, 2 * LANES * p + LANES, 2 * LANES * (p + 1)
        q_ref[0, :, e0:e1] = jnp.where(lo_half, qn, augq[:, e0:e1]).astype(BF16)
        q_ref[0, :, e1:e2] = jnp.where(lo_half, augq[:, e1:e2], qn).astype(BF16)
        k_ref[0, :, e0:e1] = jnp.where(lo_half, kn, augk[:, e0:e1]).astype(BF16)
        k_ref[0, :, e1:e2] = jnp.where(lo_half, augk[:, e1:e2], kn).astype(BF16)
        v_even, v_odd = _value_blocks(proj[:, 2 * d + LANES * p:2 * d + LANES * (p + 1)], lo_half, lane)
        v_ref[0, :, e0:e1] = v_even
        v_ref[0, :, e1:e2] = v_odd


def _fox_selectors():
    selq = np.zeros((LANES, HEADS * LANES), np.float32)
    selk = np.zeros((LANES, HEADS * LANES), np.float32)
    for h in range(HEADS):
        base = h * LANES + (HEAD_DIM if h % 2 == 0 else 0)
        for j, src in enumerate((CUM_HI, CUM_MID, CUM_LO)):
            selq[src + h, base + j] = 1.0
            selq[CUM_ONE, base + 3 + j] = 1.0
            selk[CUM_ONE, base + j] = 1.0
            selk[src + h, base + 3 + j] = -1.0
        for j in range(2):
            selq[CUM_ONE + 1 + j, base + 6 + j] = 1.0
            selk[CUM_ONE, base + 6 + j] = 1.0
    return jnp.asarray(selq, BF16), jnp.asarray(selk, BF16)


def _score_bound(q_gain, k_gain, dim):
    bound = (jnp.max(jnp.abs(q_gain)) * jnp.max(jnp.abs(k_gain)) * (LOG2E * math.sqrt(dim))).astype(F32)
    usable = bound <= MAX_FAST_BOUND
    bound = jnp.where(usable, bound, 0.0)
    hi = bound.astype(BF16).astype(F32)
    return usable, hi, bound - hi


def _fox_pre(x, mod, layer, nrm, w_in, b_f, q_norm, k_norm, bound_hi, bound_lo):
    bsz, s, d = x.shape
    vec = lambda which: _mod_spec(layer, which, bsz, lambda b, t: b)
    tm = TM_PRE
    wqkv = w_in.astype(BF16)
    wf = w_in[:, 3 * d:]
    zpad = jnp.zeros((d, LANES - 3 * HEADS), F32)
    wf3 = jnp.concatenate([wf, wf, wf, zpad], axis=1).astype(BF16)
    bf3 = jnp.concatenate([b_f, b_f, b_f, jnp.zeros((LANES - 3 * HEADS,), F32)]).reshape(1, LANES)
    gq = jnp.tile(q_norm, 2).reshape(1, LANES)
    gk = jnp.tile(k_norm, 2).reshape(1, LANES)
    selq, selk = _fox_selectors()
    off = jnp.concatenate([jnp.zeros((CUM_ONE,), F32), jnp.ones((1,), F32), -bound_hi[None], -bound_lo[None],
                           jnp.zeros((LANES - CUM_ONE - 3,), F32)]).reshape(1, LANES)
    full = lambda shape: pl.BlockSpec(shape, lambda b, t: (0,) * len(shape))
    hq = HEADS * LANES
    head_blocks = pl.BlockSpec((1, tm, hq), lambda b, t: (b, t, 0))
    return pl.pallas_call(
        _fox_pre_kernel,
        grid=(bsz, s // tm),
        in_specs=[
            pl.BlockSpec((1, tm, d), lambda b, t: (b, t, 0)),
            vec(MOD_SHIFT1), vec(MOD_SCALE1), full((1, d)),
            full((d, 3 * d)), full((d, LANES)), full((1, LANES)), full((1, LANES)), full((1, LANES)),
            full((LANES, hq)), full((LANES, hq)), full((1, LANES)),
        ],
        out_specs=[head_blocks] * 3 + [pl.BlockSpec((1, 1, 8, LANES), lambda b, t: (b, t, 0, 0))],
        out_shape=[jax.ShapeDtypeStruct((bsz, s, hq), BF16)] * 3
        + [jax.ShapeDtypeStruct((bsz, s // tm, 8, LANES), F32)],
        scratch_shapes=[pltpu.VMEM((1, LANES), F32)],
        compiler_params=_params("arbitrary", "arbitrary"),
        name="fox_pre",
    )(x, mod, mod, nrm.reshape(1, d), wqkv, wf3, bf3, gq, gk, selq, selk, off)


def _mla_pre_kernel(x_ref, pos_ref, sh_ref, sc_ref, nrm_ref, win_ref, gql_ref, gkvl_ref,
                    wuq_ref, wuqs_ref, wkn_ref, wv_ref, gq_ref, gqs_ref, gk_ref, invf_ref, qoff_ref, koff_ref,
                    q_ref, k_ref, v_ref):
    tm = x_ref.shape[1]
    h = _modulated_norm(x_ref[0], nrm_ref[...], sc_ref[0], sh_ref[0])
    proj = jnp.dot(h.astype(BF16), win_ref[...], preferred_element_type=F32)
    cq = proj[:, :MLA_Q_RANK]
    ckv = proj[:, MLA_Q_RANK:MLA_Q_RANK + MLA_KV_RANK]
    kr = proj[:, MLA_Q_RANK + MLA_KV_RANK:]
    cqn = cq * lax.rsqrt(jnp.mean(cq * cq, axis=-1, keepdims=True) + EPS) * gql_ref[...]
    ckvn = ckv * lax.rsqrt(jnp.mean(ckv * ckv, axis=-1, keepdims=True) + EPS) * gkvl_ref[...]
    cqb = cqn.astype(BF16)
    ckvb = ckvn.astype(BF16)
    qall = jnp.dot(cqb, wuq_ref[...], preferred_element_type=F32)
    qswap = jnp.dot(cqb, wuqs_ref[...], preferred_element_type=F32)
    knall = jnp.dot(ckvb, wkn_ref[...], preferred_element_type=F32)
    vall = jnp.dot(ckvb, wv_ref[...], preferred_element_type=F32)

    lane = lax.broadcasted_iota(jnp.int32, (tm, LANES), 1)
    lo_half = lane < HEAD_DIM
    for p in range(PAIRS):
        v_even, v_odd = _value_blocks(vall[:, LANES * p:LANES * (p + 1)], lo_half, lane)
        v_ref[0, :, 2 * LANES * p:2 * LANES * p + LANES] = v_even
        v_ref[0, :, 2 * LANES * p + LANES:2 * LANES * (p + 1)] = v_odd
    first_half = lane < HEAD_DIM + MLA_ROPE // 2
    ang = pos_ref[0] * invf_ref[...]
    cosf = jnp.cos(ang)
    sinf = jnp.sin(ang)
    sins = jnp.where(first_half, -sinf, sinf)
    half = MLA_ROPE // 2

    def swap_halves(t):
        return jnp.where(first_half, pltpu.roll(t, LANES - half, axis=1), pltpu.roll(t, half, axis=1))

    gk = gk_ref[...]
    q_cos = gq_ref[...] * cosf
    q_sin = gqs_ref[...] * sins
    k_cos = gk * cosf
    sw_k = swap_halves(kr * gk) * sins
    inv_dim = 1.0 / MLA_QK
    for hd in range(HEADS):
        sl = slice(LANES * hd, LANES * (hd + 1))
        qh = qall[:, sl]
        rq = lax.rsqrt(jnp.sum(qh * qh, axis=-1, keepdims=True) * inv_dim + EPS)
        qf = (qh * q_cos + qswap[:, sl] * q_sin) * rq
        q_ref[0, :, sl] = (qf + qoff_ref[...]).astype(BF16)
        kh = knall[:, sl] + kr
        rk = lax.rsqrt(jnp.sum(kh * kh, axis=-1, keepdims=True) * inv_dim + EPS)
        kf = rk * (kh * k_cos + sw_k)
        k_ref[0, :, sl] = (kf + koff_ref[...]).astype(BF16)


def _mla_pre(x, positions, mod, layer, nrm, w_in, q_lat_norm, kv_lat_norm, w_uq, w_ukv, q_norm, k_norm,
             bound_hi, bound_lo):
    bsz, s, d = x.shape
    tm = TM_PRE
    hq = HEADS * LANES
    lat = MLA_Q_RANK + MLA_KV_RANK
    half = MLA_ROPE // 2
    r0, r1, r2 = HEAD_DIM, HEAD_DIM + half, HEAD_DIM + MLA_ROPE

    def widen(t, left):
        return jnp.pad(t, [(0, 0)] * (t.ndim - 1) + [(left, LANES - left - t.shape[-1])])

    def swap_rotary(t):
        return widen(jnp.concatenate([t[..., r1:r2], t[..., r0:r1]], axis=-1), r0)

    win = jnp.concatenate([w_in[:, :lat], widen(w_in[:, lat:], r0)], axis=1).astype(BF16)
    uq = w_uq.reshape(MLA_Q_RANK, HEADS, MLA_QK)
    wuq = widen(uq, 0).reshape(MLA_Q_RANK, hq).astype(BF16)
    wuqs = swap_rotary(uq).reshape(MLA_Q_RANK, hq).astype(BF16)
    ukv = w_ukv.reshape(MLA_KV_RANK, HEADS, 2 * HEAD_DIM)
    wkn = widen(ukv[:, :, :HEAD_DIM], 0).reshape(MLA_KV_RANK, hq).astype(BF16)
    wv = ukv[:, :, HEAD_DIM:].reshape(MLA_KV_RANK, HEADS * HEAD_DIM).astype(BF16)
    gq = widen(q_norm, 0).reshape(1, LANES) * (MLA_QK ** -0.5 * LOG2E)
    gqs = swap_rotary(gq)
    gk = widen(k_norm, 0).reshape(1, LANES)
    inv_freq = ROPE_THETA ** (-jnp.arange(0, half, dtype=F32) / half)
    invf = widen(jnp.concatenate([inv_freq, inv_freq]), r0).reshape(1, LANES)
    pos = positions.astype(F32).reshape(bsz, s, 1)
    qoff = widen(jnp.stack([-bound_hi, -bound_lo]), MLA_QK).reshape(1, LANES)
    koff = jnp.asarray(np.pad(np.ones((1, 2), np.float32), ((0, 0), (MLA_QK, LANES - MLA_QK - 2))))
    full = lambda shape: pl.BlockSpec(shape, lambda b, t: (0,) * len(shape))
    vec = lambda which: _mod_spec(layer, which, bsz, lambda b, t: b)
    head_blocks = pl.BlockSpec((1, tm, hq), lambda b, t: (b, t, 0))
    return pl.pallas_call(
        _mla_pre_kernel,
        grid=(bsz, s // tm),
        in_specs=[
            pl.BlockSpec((1, tm, d), lambda b, t: (b, t, 0)),
            pl.BlockSpec((1, tm, 1), lambda b, t: (b, t, 0)),
            vec(MOD_SHIFT1), vec(MOD_SCALE1), full((1, d)),
            full((d, lat + LANES)), full((1, MLA_Q_RANK)), full((1, MLA_KV_RANK)),
            full((MLA_Q_RANK, hq)), full((MLA_Q_RANK, hq)), full((MLA_KV_RANK, hq)), full((MLA_KV_RANK, d)),
            full((1, LANES)), full((1, LANES)), full((1, LANES)), full((1, LANES)), full((1, LANES)),
            full((1, LANES)),
        ],
        out_specs=[head_blocks] * 3,
        out_shape=[jax.ShapeDtypeStruct((bsz, s, hq), BF16)] * 3,
        compiler_params=_params("arbitrary", "arbitrary"),
        name="mla_pre",
    )(x, pos, mod, mod, nrm.reshape(1, d), win, q_lat_norm.reshape(1, -1), kv_lat_norm.reshape(1, -1),
      wuq, wuqs, wkn, wv, gq, gqs, gk, invf, qoff, koff)


def _attn_tiles(q_ref, k_ref, v_ref, hh, j):
    hs = slice(LANES * hh, LANES * (hh + 1))
    start = pl.multiple_of(j * TK, TK)
    return k_ref[0, pl.ds(start, TK), hs], v_ref[0, pl.ds(start, TK), hs]


def _causal_mask(tq):
    row = lax.broadcasted_iota(jnp.int32, (tq, TK), 0)
    col = lax.broadcasted_iota(jnp.int32, (tq, TK), 1)
    return row >= col


def _attn_online_kernel(q_ref, k_ref, v_ref, o_ref):
    tq = q_ref.shape[1]
    qi = pl.program_id(2)
    causal = _causal_mask(tq)
    qs = [q_ref[0, :, LANES * hh:LANES * (hh + 1)] for hh in range(2)]

    def step(j, carry, masked):
        new = []
        for hh in range(2):
            m, l, acc = carry[hh]
            k, v = _attn_tiles(q_ref, k_ref, v_ref, hh, j)
            s = lax.dot_general(qs[hh], k, (((1,), (1,)), ((), ())), preferred_element_type=F32)
            if masked:
                s = jnp.where(causal, s, -jnp.inf)
            m_new = jnp.maximum(m, jnp.max(s, axis=-1, keepdims=True))
            alpha = jnp.exp2(m - m_new)
            p = jnp.exp2(s - m_new)
            l_new = alpha * l + jnp.sum(p, axis=-1, keepdims=True)
            acc_new = alpha * acc + jnp.dot(p.astype(BF16), v, preferred_element_type=F32)
            new.append((m_new, l_new, acc_new))
        return tuple(new)

    one = (jnp.full((tq, 1), -jnp.inf, F32), jnp.zeros((tq, 1), F32), jnp.zeros((tq, LANES), F32))
    carry = lax.fori_loop(0, qi, functools.partial(step, masked=False), (one, one))
    (_, l0, a0), (_, l1, a1) = step(qi, carry, True)
    lane = lax.broadcasted_iota(jnp.int32, (tq, LANES), 1)
    o_ref[0] = jnp.where(lane < HEAD_DIM, a0 / l0, a1 / l1).astype(BF16)


def _first_key_tile(edge_ref, qi, group, nh):
    first = edge_ref[0, qi, 0:1, :]
    skipped = jnp.zeros((1, LANES), F32)
    for j in range(edge_ref.shape[1]):
        skipped += jnp.where(first - edge_ref[0, j, 1:2, :] <= UNDERFLOW_LOG2, 1.0, 0.0)
    lane = lax.broadcasted_iota(jnp.int32, (1, LANES), 1)
    in_group = jnp.logical_and(lane >= group * nh, lane < (group + 1) * nh)
    return jnp.min(jnp.where(in_group, skipped, float(edge_ref.shape[1]))).astype(jnp.int32)


def _attn_bounded_kernel(*refs, forgetting):
    q_ref, k_ref, v_ref, o_ref, acc_ref = refs[-5:]
    tq = q_ref.shape[1]
    nh = q_ref.shape[2] // LANES
    qi = pl.program_id(2)
    j0 = _first_key_tile(refs[0], qi, pl.program_id(1), nh) if forgetting else 0
    causal = _causal_mask(tq)
    qs = [q_ref[0, :, LANES * hh:LANES * (hh + 1)] for hh in range(nh)]

    def tiles(js, masked_last):
        sums = [None] * nh
        for idx, j in enumerate(js):
            for hh in range(nh):
                k, v = _attn_tiles(q_ref, k_ref, v_ref, hh, j)
                s = lax.dot_general(qs[hh], k, (((1,), (1,)), ((), ())), preferred_element_type=F32)
                if masked_last and idx == len(js) - 1:
                    s = jnp.where(causal, s, -jnp.inf)
                pv = jnp.dot(jnp.exp2(s).astype(BF16), v, preferred_element_type=F32)
                sums[hh] = pv if sums[hh] is None else sums[hh] + pv
        return sums

    def trips(n, base):
        def body(jj, carry):
            for hh, part in enumerate(tiles([base + n * jj + u for u in range(n)], False)):
                acc_ref[hh] += part
            return carry
        return body

    def finish(js):
        sums = tiles(js, True)
        lane = lax.broadcasted_iota(jnp.int32, (tq, LANES), 1)
        for p in range(nh // 2):
            a0, a1 = acc_ref[2 * p] + sums[2 * p], acc_ref[2 * p + 1] + sums[2 * p + 1]
            o_ref[0, :, LANES * p:LANES * (p + 1)] = jnp.where(
                lane < HEAD_DIM, a0 / a0[:, HEAD_DIM:HEAD_DIM + 1], a1 / a1[:, 0:1]).astype(BF16)

    n = qi - j0
    acc_ref[...] = jnp.zeros_like(acc_ref)
    lax.fori_loop(0, n >> 2, trips(4, j0), 0)
    for left in range(4):
        @pl.when((n & 3) == left)
        def _(left=left):
            finish([qi - left + u for u in range(left + 1)])


def _attention(q, k, v, bounded, edges, heads_per_step):
    bsz, s, _ = q.shape
    assert TQ == TK

    def call(body, name, nh, is_bounded):
        w = nh * LANES
        in_specs = [
            pl.BlockSpec((1, TQ, w), lambda b, g, i: (b, i, g)),
            pl.BlockSpec((1, s, w), lambda b, g, i: (b, 0, g)),
            pl.BlockSpec((1, s, w), lambda b, g, i: (b, 0, g)),
        ]
        scratch = []
        if is_bounded:
            scratch = [pltpu.VMEM((nh, TQ, LANES), F32)]
            if edges is not None:
                in_specs.insert(0, pl.BlockSpec((1,) + edges.shape[1:], lambda b, g, i: (b, 0, 0, 0)))
        return pl.pallas_call(
            body,
            grid=(bsz, HEADS // nh, s // TQ),
            in_specs=in_specs,
            out_specs=pl.BlockSpec((1, TQ, w // 2), lambda b, g, i: (b, i, g)),
            out_shape=jax.ShapeDtypeStruct((bsz, s, PAIRS * LANES), BF16),
            scratch_shapes=scratch,
            compiler_params=_params("arbitrary", "arbitrary", "arbitrary"),
            name=name,
        )

    fast = call(functools.partial(_attn_bounded_kernel, forgetting=edges is not None), "attn_bounded",
                heads_per_step, True)
    online = call(_attn_online_kernel, "attn_online", 2, False)
    if edges is None:
        return lax.cond(bounded, fast, online, q, k, v)
    return lax.cond(bounded, fast, lambda e, *qkv: online(*qkv), edges, q, k, v)


def _post_kernel(o_ref, x_ref, g1_ref, sh_ref, sc_ref, nrm_ref, wout_ref, wrhl_ref, br_ref,
                 x1_ref, h2_ref, comb_ref):
    tm = x_ref.shape[1]
    y = jnp.dot(o_ref[0], wout_ref[...], preferred_element_type=F32)
    x1 = x_ref[0] + g1_ref[0] * y
    x1_ref[0] = x1
    h2 = _modulated_norm(x1, nrm_ref[...], sc_ref[0], sh_ref[0])
    h_hi = h2.astype(BF16)
    h2_ref[0] = h_hi
    h_lo = (h2 - h_hi.astype(F32)).astype(BF16)
    hw = jnp.dot(h_hi, wrhl_ref[...], preferred_element_type=F32)
    lg = (hw[:, :LANES] + hw[:, LANES:]
          + jnp.dot(h_lo, wrhl_ref[:, :LANES], preferred_element_type=F32)) + br_ref[...]
    lane = lax.broadcasted_iota(jnp.int32, (tm, LANES), 1)
    lanef = lane.astype(F32)
    neg = -jnp.inf
    is_g = jnp.logical_and(lane >= N_EXPERTS, lane < N_EXPERTS + N_GROUPS)
    gl = jnp.where(is_g, lg, neg)
    gmax = jnp.max(gl, axis=-1, keepdims=True)
    gsum = jnp.sum(jnp.exp(gl - gmax), axis=-1, keepdims=True)
    g_top = 1.0 / gsum
    gidx = jnp.min(jnp.where(gl == gmax, lanef - N_EXPERTS, 1e9), axis=-1, keepdims=True)
    lane_grp = (lane >> 2).astype(F32)
    in_grp = jnp.logical_and(lane < N_EXPERTS, lane_grp == gidx)
    el = jnp.where(in_grp, lg, neg)
    e1 = jnp.max(el, axis=-1, keepdims=True)
    i1 = jnp.min(jnp.where(el == e1, lanef, 1e9), axis=-1, keepdims=True)
    el2 = jnp.where(lanef == i1, neg, el)
    e2 = jnp.max(el2, axis=-1, keepdims=True)
    i2 = jnp.min(jnp.where(el2 == e2, lanef, 1e9), axis=-1, keepdims=True)
    t = jnp.exp(e2 - e1)
    w1 = g_top / (1.0 + t)
    w2 = g_top * t / (1.0 + t)
    comb_ref[0] = jnp.where(lanef == i1, w1, jnp.where(lanef == i2, w2, 0.0))


def _post(o, x, mod, layer, nrm, w_out, w_grp, b_grp, w_rt, b_rt):
    bsz, s, d = x.shape
    tm = TM_POST
    unused = LANES - N_EXPERTS - N_GROUPS
    wr = jnp.concatenate([w_rt, w_grp, jnp.zeros((d, unused), F32)], axis=1)
    wrh = wr.astype(BF16)
    wrhl = jnp.concatenate([wrh, (wr - wrh.astype(F32)).astype(BF16)], axis=1)
    br = jnp.concatenate([b_rt, b_grp, jnp.zeros((unused,), F32)]).reshape(1, LANES)
    full = lambda shape: pl.BlockSpec(shape, lambda b, t: (0,) * len(shape))
    vec = lambda which: _mod_spec(layer, which, bsz, lambda b, t: b)
    tile = lambda w: pl.BlockSpec((1, tm, w), lambda b, t: (b, t, 0))
    return pl.pallas_call(
        _post_kernel,
        grid=(bsz, s // tm),
        in_specs=[tile(d), tile(d), vec(MOD_GATE1), vec(MOD_SHIFT2), vec(MOD_SCALE2), full((1, d)), full((d, d)),
                  full((d, 2 * LANES)), full((1, LANES))],
        out_specs=[tile(d), tile(d), tile(LANES)],
        out_shape=[
            jax.ShapeDtypeStruct((bsz, s, d), F32),
            jax.ShapeDtypeStruct((bsz, s, d), BF16),
            jax.ShapeDtypeStruct((bsz, s, LANES), F32),
        ],
        compiler_params=_params("arbitrary", "arbitrary"),
        name="post_router",
    )(o, x, mod, mod, mod, nrm.reshape(1, d), w_out.astype(BF16), wrhl, br)


def _moe_kernel(h_ref, comb_ref, x1_ref, g2_ref, wg_ref, wu_ref, wd_ref, out_ref):
    tm = h_ref.shape[0]
    h = h_ref[...]
    comb = comb_ref[...]
    lane = lax.broadcasted_iota(jnp.int32, (tm, LANES), 1)
    total = None
    for e in range(N_EXPERTS):
        a = jnp.dot(h, wg_ref[0, e], preferred_element_type=F32)
        u = jnp.dot(h, wu_ref[0, e], preferred_element_type=F32)
        cw = jnp.sum(jnp.where(lane == e, comb, 0.0), axis=-1, keepdims=True)
        hid = (a * _sigmoid(a)) * u * cw
        part = jnp.dot(hid.astype(BF16), wd_ref[0, e], preferred_element_type=F32)
        total = part if total is None else total + part
    out_ref[...] = x1_ref[...] + g2_ref[0] * total


def _moe(h2, comb, x1, mod, w_gate, w_up, w_down, layer):
    bsz, s, d = x1.shape
    t = bsz * s
    tm = TM_MOE
    per_batch = s // tm
    resident = lambda shape: pl.BlockSpec((1, N_EXPERTS) + shape, lambda i: (layer, 0, 0, 0),
                                          pipeline_mode=pl.Buffered(1))
    out = pl.pallas_call(
        _moe_kernel,
        grid=(t // tm,),
        in_specs=[
            pl.BlockSpec((tm, d), lambda i: (i, 0)),
            pl.BlockSpec((tm, LANES), lambda i: (i, 0)),
            pl.BlockSpec((tm, d), lambda i: (i, 0)),
            _mod_spec(layer, MOD_GATE2, bsz, lambda i: i // per_batch),
            resident((d, D_EXPERT)), resident((d, D_EXPERT)), resident((D_EXPERT, d)),
        ],
        out_specs=pl.BlockSpec((tm, d), lambda i: (i, 0)),
        out_shape=jax.ShapeDtypeStruct((t, d), F32),
        compiler_params=_params("arbitrary"),
        name="moe_dense",
    )(h2.reshape(t, d), comb.reshape(t, LANES), x1.reshape(t, d), mod, w_gate, w_up, w_down)
    return out.reshape(bsz, s, d)


def kernel(x, c, positions, w_ada, b_ada, mix_norm, ffn_norm, fox_w_in, fox_b_f, fox_q_norm, fox_k_norm,
           fox_w_out, mla_w_in, mla_q_lat_norm, mla_kv_lat_norm, mla_w_uq, mla_w_ukv, mla_q_norm, mla_k_norm,
           mla_w_out, moe_w_grp, moe_b_grp, moe_w_rt, moe_b_rt, moe_w_gate, moe_w_up, moe_w_down):
    depth = w_ada.shape[0]
    d = x.shape[-1]
    mod = _ada_modulation(c, w_ada, b_ada).reshape(depth * x.shape[0] * 6, 1, d)
    wg, wu, wd = moe_w_gate.astype(BF16), moe_w_up.astype(BF16), moe_w_down.astype(BF16)
    for i in range(depth):
        j = i // 2
        if i % 2 == 0:
            bounded, b_hi, b_lo = _score_bound(fox_q_norm[j], fox_k_norm[j], HEAD_DIM)
            q, k, v, edges = _fox_pre(x, mod, i, mix_norm[i], fox_w_in[j], fox_b_f[j], fox_q_norm[j],
                                      fox_k_norm[j], b_hi, b_lo)
            nh = FOX_HEADS_PER_STEP
            w_out = fox_w_out[j]
        else:
            bounded, b_hi, b_lo = _score_bound(mla_q_norm[j], mla_k_norm[j], MLA_QK)
            q, k, v = _mla_pre(x, positions, mod, i, mix_norm[i], mla_w_in[j], mla_q_lat_norm[j],
                               mla_kv_lat_norm[j], mla_w_uq[j], mla_w_ukv[j], mla_q_norm[j], mla_k_norm[j],
                               b_hi, b_lo)
            nh = MLA_HEADS_PER_STEP
            edges = None
            w_out = mla_w_out[j]
        o = _attention(q, k, v, bounded, edges, nh)
        x1, h2, comb = _post(o, x, mod, i, ffn_norm[i], w_out, moe_w_grp[i], moe_b_grp[i],
                             moe_w_rt[i], moe_b_rt[i])
        x = _moe(h2, comb, x1, mod, wg, wu, wd, i)
    return x
```

```python
import functools
import math

import numpy as np
import jax
import jax.numpy as jnp
from jax import lax
from jax.experimental import pallas as pl
from jax.experimental.pallas import tpu as pltpu

F32 = jnp.float32
BF16 = jnp.bfloat16

D_MODEL = 1024
HEADS = 16
HEAD_DIM = 64
LANES = 128
PAIRS = HEADS // 2
EPS = 1e-6
MLA_Q_RANK = 384
MLA_KV_RANK = 256
MLA_ROPE = 32
MLA_QK = HEAD_DIM + MLA_ROPE
ROPE_THETA = 10000.0
N_GROUPS = 4
EPG = 4
EPG_SHIFT = EPG.bit_length() - 1
N_EXPERTS = N_GROUPS * EPG
assert EPG == 1 << EPG_SHIFT
D_EXPERT = 256

VMEM_LIMIT = 56 * 1024 * 1024
LOG2E = math.log2(math.e)
MAX_FAST_BOUND = 40.0
UNDERFLOW_LOG2 = -136.0
FOX_HEADS_PER_STEP = 2
MLA_HEADS_PER_STEP = 4

TM_PRE = 512
TM_POST = 512
TM_MOE = 1024
TQ = 512
TK = 512
assert TM_PRE == TQ == TK

CUM_HI, CUM_MID, CUM_LO, CUM_ONE = 0, 16, 32, 48


def _params(*sem):
    return pltpu.CompilerParams(dimension_semantics=sem, vmem_limit_bytes=VMEM_LIMIT)


def _sigmoid(x):
    return 1.0 / (1.0 + jnp.exp(-x))


MOD_SHIFT1, MOD_SCALE1, MOD_GATE1, MOD_SHIFT2, MOD_SCALE2, MOD_GATE2 = range(6)


def _mod_spec(layer, which, bsz, batch_of):
    return pl.BlockSpec((1, 1, D_MODEL), lambda *g: ((layer * bsz + batch_of(*g)) * 6 + which, 0, 0))


def _modulated_norm(x, gain, scale, shift):
    ms = jnp.mean(x * x, axis=-1, keepdims=True)
    return (x * lax.rsqrt(ms + EPS) * gain) * (1.0 + scale) + shift


def _ada_kernel(c_ref, w_ref, b_ref, o_ref):
    w = w_ref[0]
    for b in range(c_ref.shape[0]):
        c = c_ref[b]
        act = c * _sigmoid(c)
        o_ref[0, b] = jnp.sum(act * w, axis=0, keepdims=True) + b_ref[0]


def _ada_modulation(c, w_ada, b_ada):
    depth, d, n = w_ada.shape
    bsz = c.shape[0]
    tn = 1536
    out = pl.pallas_call(
        _ada_kernel,
        grid=(depth, n // tn),
        in_specs=[
            pl.BlockSpec((bsz, d, 1), lambda i, j: (0, 0, 0)),
            pl.BlockSpec((1, d, tn), lambda i, j: (i, 0, j)),
            pl.BlockSpec((1, 1, tn), lambda i, j: (i, 0, j)),
        ],
        out_specs=pl.BlockSpec((1, bsz, 1, tn), lambda i, j: (i, 0, 0, j)),
        out_shape=jax.ShapeDtypeStruct((depth, bsz, 1, n), F32),
        compiler_params=_params("arbitrary", "arbitrary"),
        name="ada_mod",
    )(c.reshape(bsz, d, 1), w_ada, b_ada.reshape(depth, 1, n))
    return out.reshape(depth, bsz, n)


def _pair_rmsnorm(xp, gain, lo_half):
    sq = xp * xp
    s_all = jnp.sum(sq, axis=-1, keepdims=True)
    s_lo = jnp.sum(jnp.where(lo_half, sq, 0.0), axis=-1, keepdims=True)
    ms = jnp.where(lo_half, s_lo, s_all - s_lo) * (1.0 / HEAD_DIM)
    return xp * lax.rsqrt(ms + EPS) * gain


def _value_blocks(vp, lo_half, lane):
    even = jnp.where(lo_half, vp, jnp.where(lane == HEAD_DIM, 1.0, 0.0))
    odd = jnp.where(lo_half, jnp.where(lane == 0, 1.0, 0.0), vp)
    return even.astype(BF16), odd.astype(BF16)


def _fox_pre_kernel(x_ref, sh_ref, sc_ref, nrm_ref, wqkv_ref, wf_ref, bf_ref, gq_ref, gk_ref,
                    selq_ref, selk_ref, off_ref, q_ref, k_ref, v_ref, edge_ref, carry_ref):
    tm = x_ref.shape[1]

    @pl.when(pl.program_id(1) == 0)
    def _():
        carry_ref[...] = jnp.zeros_like(carry_ref)

    h = _modulated_norm(x_ref[0], nrm_ref[...], sc_ref[0], sh_ref[0])
    hb = h.astype(BF16)
    proj = jnp.dot(hb, wqkv_ref[...], preferred_element_type=F32)
    fl = jnp.dot(hb, wf_ref[...], preferred_element_type=F32) + bf_ref[...]
    logf = jnp.minimum(fl, 0.0) - jnp.log(1.0 + jnp.exp(-jnp.abs(fl)))

    row = lax.broadcasted_iota(jnp.int32, (tm, tm), 0)
    col = lax.broadcasted_iota(jnp.int32, (tm, tm), 1)
    tri = jnp.where(row >= col, 1.0, 0.0).astype(BF16)
    l_hi = logf.astype(BF16)
    l_r = logf - l_hi.astype(F32)
    l_mid = l_r.astype(BF16)
    l_lo = (l_r - l_mid.astype(F32)).astype(BF16)
    cum = (jnp.dot(tri, l_hi, preferred_element_type=F32) + jnp.dot(tri, l_mid, preferred_element_type=F32)
           + jnp.dot(tri, l_lo, preferred_element_type=F32)) + carry_ref[...]
    carry_ref[...] = cum[tm - 1:tm, :]

    lane = lax.broadcasted_iota(jnp.int32, (tm, LANES), 1)
    cum2 = cum * LOG2E
    edge_ref[0, 0, 0:1, :] = cum2[0:1, :]
    edge_ref[0, 0, 1:2, :] = cum2[tm - 1:tm, :]
    edge_ref[0, 0, 2:8, :] = jnp.zeros((6, LANES), F32)
    hi = cum2.astype(BF16).astype(F32)
    r1 = cum2 - hi
    mid = r1.astype(BF16).astype(F32)
    lo = r1 - mid
    parts = jnp.where(lane < CUM_MID, hi,
                      jnp.where(lane < CUM_LO, mid,
                                jnp.where(lane < CUM_ONE, lo, off_ref[...])))
    pb = parts.astype(BF16)
    augq = jnp.dot(pb, selq_ref[...], preferred_element_type=F32)
    augk = jnp.dot(pb, selk_ref[...], preferred_element_type=F32)

    lo_half = lane < HEAD_DIM
    d = D_MODEL
    for p in range(PAIRS):
        qn = _pair_rmsnorm(proj[:, LANES * p:LANES * (p + 1)], gq_ref[...], lo_half) * (HEAD_DIM ** -0.5 * LOG2E)
        kn = _pair_rmsnorm(proj[:, d + LANES * p:d + LANES * (p + 1)], gk_ref[...], lo_half)
        e0, e1, e2 = 2 * LANES * p, 2 * LANES * p + LANES, 2 * LANES * (p + 1)
        q_ref[0, :, e0:e1] = jnp.where(lo_half, qn, augq[:, e0:e1]).astype(BF16)
        q_ref[0, :, e1:e2] = jnp.where(lo_half, augq[:, e1:e2], qn).astype(BF16)
        k_ref[0, :, e0:e1] = jnp.where(lo_half, kn, augk[:, e0:e1]).astype(BF16)
        k_ref[0, :, e1:e2] = jnp.where(lo_half, augk[:, e1:e2], kn).astype(BF16)
        v_even, v_odd = _value_blocks(proj[:, 2 * d + LANES * p:2 * d + LANES * (p + 1)], lo_half, lane)
        v_ref[0, :, e0:e1] = v_even
        v_ref[0, :, e1:e2] = v_odd


def _fox_selectors():
    selq = np.zeros((LANES, HEADS * LANES), np.float32)
    selk = np.zeros((LANES, HEADS * LANES), np.float32)
    for h in range(HEADS):
        base = h * LANES + (HEAD_DIM if h % 2 == 0 else 0)
        for j, src in enumerate((CUM_HI, CUM_MID, CUM_LO)):
            selq[src + h, base + j] = 1.0
            selq[CUM_ONE, base + 3 + j] = 1.0
            selk[CUM_ONE, base + j] = 1.0
            selk[src + h, base + 3 + j] = -1.0
        for j in range(2):
            selq[CUM_ONE + 1 + j, base + 6 + j] = 1.0
            selk[CUM_ONE, base + 6 + j] = 1.0
    return jnp.asarray(selq, BF16), jnp.asarray(selk, BF16)


def _score_bound(q_gain, k_gain, dim):
    bound = (jnp.max(jnp.abs(q_gain)) * jnp.max(jnp.abs(k_gain)) * (LOG2E * math.sqrt(dim))).astype(F32)
    usable = bound <= MAX_FAST_BOUND
    bound = jnp.where(usable, bound, 0.0)
    hi = bound.astype(BF16).astype(F32)
    return usable, hi, bound - hi


def _fox_pre(x, mod, layer, nrm, w_in, b_f, q_norm, k_norm, bound_hi, bound_lo):
    bsz, s, d = x.shape
    vec = lambda which: _mod_spec(layer, which, bsz, lambda b, t: b)
    tm = TM_PRE
    wqkv = w_in.astype(BF16)
    wf = w_in[:, 3 * d:]
    zpad = jnp.zeros((d, LANES - 3 * HEADS), F32)
    wf3 = jnp.concatenate([wf, wf, wf, zpad], axis=1).astype(BF16)
    bf3 = jnp.concatenate([b_f, b_f, b_f, jnp.zeros((LANES - 3 * HEADS,), F32)]).reshape(1, LANES)
    gq = jnp.tile(q_norm, 2).reshape(1, LANES)
    gk = jnp.tile(k_norm, 2).reshape(1, LANES)
    selq, selk = _fox_selectors()
    off = jnp.concatenate([jnp.zeros((CUM_ONE,), F32), jnp.ones((1,), F32), -bound_hi[None], -bound_lo[None],
                           jnp.zeros((LANES - CUM_ONE - 3,), F32)]).reshape(1, LANES)
    full = lambda shape: pl.BlockSpec(shape, lambda b, t: (0,) * len(shape))
    hq = HEADS * LANES
    head_blocks = pl.BlockSpec((1, tm, hq), lambda b, t: (b, t, 0))
    return pl.pallas_call(
        _fox_pre_kernel,
        grid=(bsz, s // tm),
        in_specs=[
            pl.BlockSpec((1, tm, d), lambda b, t: (b, t, 0)),
            vec(MOD_SHIFT1), vec(MOD_SCALE1), full((1, d)),
            full((d, 3 * d)), full((d, LANES)), full((1, LANES)), full((1, LANES)), full((1, LANES)),
            full((LANES, hq)), full((LANES, hq)), full((1, LANES)),
        ],
        out_specs=[head_blocks] * 3 + [pl.BlockSpec((1, 1, 8, LANES), lambda b, t: (b, t, 0, 0))],
        out_shape=[jax.ShapeDtypeStruct((bsz, s, hq), BF16)] * 3
        + [jax.ShapeDtypeStruct((bsz, s // tm, 8, LANES), F32)],
        scratch_shapes=[pltpu.VMEM((1, LANES), F32)],
        compiler_params=_params("arbitrary", "arbitrary"),
        name="fox_pre",
    )(x, mod, mod, nrm.reshape(1, d), wqkv, wf3, bf3, gq, gk, selq, selk, off)


def _mla_pre_kernel(x_ref, pos_ref, sh_ref, sc_ref, nrm_ref, win_ref, gql_ref, gkvl_ref,
                    wuq_ref, wuqs_ref, wkn_ref, wv_ref, gq_ref, gqs_ref, gk_ref, invf_ref, qoff_ref, koff_ref,
                    q_ref, k_ref, v_ref):
    tm = x_ref.shape[1]
    h = _modulated_norm(x_ref[0], nrm_ref[...], sc_ref[0], sh_ref[0])
    proj = jnp.dot(h.astype(BF16), win_ref[...], preferred_element_type=F32)
    cq = proj[:, :MLA_Q_RANK]
    ckv = proj[:, MLA_Q_RANK:MLA_Q_RANK + MLA_KV_RANK]
    kr = proj[:, MLA_Q_RANK + MLA_KV_RANK:]
    cqn = cq * lax.rsqrt(jnp.mean(cq * cq, axis=-1, keepdims=True) + EPS) * gql_ref[...]
    ckvn = ckv * lax.rsqrt(jnp.mean(ckv * ckv, axis=-1, keepdims=True) + EPS) * gkvl_ref[...]
    cqb = cqn.astype(BF16)
    ckvb = ckvn.astype(BF16)
    qall = jnp.dot(cqb, wuq_ref[...], preferred_element_type=F32)
    qswap = jnp.dot(cqb, wuqs_ref[...], preferred_element_type=F32)
    knall = jnp.dot(ckvb, wkn_ref[...], preferred_element_type=F32)
    vall = jnp.dot(ckvb, wv_ref[...], preferred_element_type=F32)

    lane = lax.broadcasted_iota(jnp.int32, (tm, LANES), 1)
    lo_half = lane < HEAD_DIM
    for p in range(PAIRS):
        v_even, v_odd = _value_blocks(vall[:, LANES * p:LANES * (p + 1)], lo_half, lane)
        v_ref[0, :, 2 * LANES * p:2 * LANES * p + LANES] = v_even
        v_ref[0, :, 2 * LANES * p + LANES:2 * LANES * (p + 1)] = v_odd
    first_half = lane < HEAD_DIM + MLA_ROPE // 2
    ang = pos_ref[0] * invf_ref[...]
    cosf = jnp.cos(ang)
    sinf = jnp.sin(ang)
    sins = jnp.where(first_half, -sinf, sinf)
    half = MLA_ROPE // 2

    def swap_halves(t):
        return jnp.where(first_half, pltpu.roll(t, LANES - half, axis=1), pltpu.roll(t, half, axis=1))

    gk = gk_ref[...]
    q_cos = gq_ref[...] * cosf
    q_sin = gqs_ref[...] * sins
    k_cos = gk * cosf
    sw_k = swap_halves(kr * gk) * sins
    inv_dim = 1.0 / MLA_QK
    for hd in range(HEADS):
        sl = slice(LANES * hd, LANES * (hd + 1))
        qh = qall[:, sl]
        rq = lax.rsqrt(jnp.sum(qh * qh, axis=-1, keepdims=True) * inv_dim + EPS)
        qf = (qh * q_cos + qswap[:, sl] * q_sin) * rq
        q_ref[0, :, sl] = (qf + qoff_ref[...]).astype(BF16)
        kh = knall[:, sl] + kr
        rk = lax.rsqrt(jnp.sum(kh * kh, axis=-1, keepdims=True) * inv_dim + EPS)
        kf = rk * (kh * k_cos + sw_k)
        k_ref[0, :, sl] = (kf + koff_ref[...]).astype(BF16)


def _mla_pre(x, positions, mod, layer, nrm, w_in, q_lat_norm, kv_lat_norm, w_uq, w_ukv, q_norm, k_norm,
             bound_hi, bound_lo):
    bsz, s, d = x.shape
    tm = TM_PRE
    hq = HEADS * LANES
    lat = MLA_Q_RANK + MLA_KV_RANK
    half = MLA_ROPE // 2
    r0, r1, r2 = HEAD_DIM, HEAD_DIM + half, HEAD_DIM + MLA_ROPE

    def widen(t, left):
        return jnp.pad(t, [(0, 0)] * (t.ndim - 1) + [(left, LANES - left - t.shape[-1])])

    def swap_rotary(t):
        return widen(jnp.concatenate([t[..., r1:r2], t[..., r0:r1]], axis=-1), r0)

    win = jnp.concatenate([w_in[:, :lat], widen(w_in[:, lat:], r0)], axis=1).astype(BF16)
    uq = w_uq.reshape(MLA_Q_RANK, HEADS, MLA_QK)
    wuq = widen(uq, 0).reshape(MLA_Q_RANK, hq).astype(BF16)
    wuqs = swap_rotary(uq).reshape(MLA_Q_RANK, hq).astype(BF16)
    ukv = w_ukv.reshape(MLA_KV_RANK, HEADS, 2 * HEAD_DIM)
    wkn = widen(ukv[:, :, :HEAD_DIM], 0).reshape(MLA_KV_RANK, hq).astype(BF16)
    wv = ukv[:, :, HEAD_DIM:].reshape(MLA_KV_RANK, HEADS * HEAD_DIM).astype(BF16)
    gq = widen(q_norm, 0).reshape(1, LANES) * (MLA_QK ** -0.5 * LOG2E)
    gqs = swap_rotary(gq)
    gk = widen(k_norm, 0).reshape(1, LANES)
    inv_freq = ROPE_THETA ** (-jnp.arange(0, half, dtype=F32) / half)
    invf = widen(jnp.concatenate([inv_freq, inv_freq]), r0).reshape(1, LANES)
    pos = positions.astype(F32).reshape(bsz, s, 1)
    qoff = widen(jnp.stack([-bound_hi, -bound_lo]), MLA_QK).reshape(1, LANES)
    koff = jnp.asarray(np.pad(np.ones((1, 2), np.float32), ((0, 0), (MLA_QK, LANES - MLA_QK - 2))))
    full = lambda shape: pl.BlockSpec(shape, lambda b, t: (0,) * len(shape))
    vec = lambda which: _mod_spec(layer, which, bsz, lambda b, t: b)
    head_blocks = pl.BlockSpec((1, tm, hq), lambda b, t: (b, t, 0))
    return pl.pallas_call(
        _mla_pre_kernel,
        grid=(bsz, s // tm),
        in_specs=[
            pl.BlockSpec((1, tm, d), lambda b, t: (b, t, 0)),
            pl.BlockSpec((1, tm, 1), lambda b, t: (b, t, 0)),
            vec(MOD_SHIFT1), vec(MOD_SCALE1), full((1, d)),
            full((d, lat + LANES)), full((1, MLA_Q_RANK)), full((1, MLA_KV_RANK)),
            full((MLA_Q_RANK, hq)), full((MLA_Q_RANK, hq)), full((MLA_KV_RANK, hq)), full((MLA_KV_RANK, d)),
            full((1, LANES)), full((1, LANES)), full((1, LANES)), full((1, LANES)), full((1, LANES)),
            full((1, LANES)),
        ],
        out_specs=[head_blocks] * 3,
        out_shape=[jax.ShapeDtypeStruct((bsz, s, hq), BF16)] * 3,
        compiler_params=_params("arbitrary", "arbitrary"),
        name="mla_pre",
    )(x, pos, mod, mod, nrm.reshape(1, d), win, q_lat_norm.reshape(1, -1), kv_lat_norm.reshape(1, -1),
      wuq, wuqs, wkn, wv, gq, gqs, gk, invf, qoff, koff)


def _attn_tiles(q_ref, k_ref, v_ref, hh, j):
    hs = slice(LANES * hh, LANES * (hh + 1))
    start = pl.multiple_of(j * TK, TK)
    return k_ref[0, pl.ds(start, TK), hs], v_ref[0, pl.ds(start, TK), hs]


def _causal_mask(tq):
    row = lax.broadcasted_iota(jnp.int32, (tq, TK), 0)
    col = lax.broadcasted_iota(jnp.int32, (tq, TK), 1)
    return row >= col


def _attn_online_kernel(q_ref, k_ref, v_ref, o_ref):
    tq = q_ref.shape[1]
    qi = pl.program_id(2)
    causal = _causal_mask(tq)
    qs = [q_ref[0, :, LANES * hh:LANES * (hh + 1)] for hh in range(2)]

    def step(j, carry, masked):
        new = []
        for hh in range(2):
            m, l, acc = carry[hh]
            k, v = _attn_tiles(q_ref, k_ref, v_ref, hh, j)
            s = lax.dot_general(qs[hh], k, (((1,), (1,)), ((), ())), preferred_element_type=F32)
            if masked:
                s = jnp.where(causal, s, -jnp.inf)
            m_new = jnp.maximum(m, jnp.max(s, axis=-1, keepdims=True))
            alpha = jnp.exp2(m - m_new)
            p = jnp.exp2(s - m_new)
            l_new = alpha * l + jnp.sum(p, axis=-1, keepdims=True)
            acc_new = alpha * acc + jnp.dot(p.astype(BF16), v, preferred_element_type=F32)
            new.append((m_new, l_new, acc_new))
        return tuple(new)

    one = (jnp.full((tq, 1), -jnp.inf, F32), jnp.zeros((tq, 1), F32), jnp.zeros((tq, LANES), F32))
    carry = lax.fori_loop(0, qi, functools.partial(step, masked=False), (one, one))
    (_, l0, a0), (_, l1, a1) = step(qi, carry, True)
    lane = lax.broadcasted_iota(jnp.int32, (tq, LANES), 1)
    o_ref[0] = jnp.where(lane < HEAD_DIM, a0 / l0, a1 / l1).astype(BF16)


def _first_key_tile(edge_ref, qi, group, nh):
    first = edge_ref[0, qi, 0:1, :]
    skipped = jnp.zeros((1, LANES), F32)
    for j in range(edge_ref.shape[1]):
        skipped += jnp.where(first - edge_ref[0, j, 1:2, :] <= UNDERFLOW_LOG2, 1.0, 0.0)
    lane = lax.broadcasted_iota(jnp.int32, (1, LANES), 1)
    in_group = jnp.logical_and(lane >= group * nh, lane < (group + 1) * nh)
    return jnp.min(jnp.where(in_group, skipped, float(edge_ref.shape[1]))).astype(jnp.int32)


def _attn_bounded_kernel(*refs, forgetting):
    q_ref, k_ref, v_ref, o_ref, acc_ref = refs[-5:]
    tq = q_ref.shape[1]
    nh = q_ref.shape[2] // LANES
    qi = pl.program_id(2)
    j0 = _first_key_tile(refs[0], qi, pl.program_id(1), nh) if forgetting else 0
    causal = _causal_mask(tq)
    qs = [q_ref[0, :, LANES * hh:LANES * (hh + 1)] for hh in range(nh)]

    def tiles(js, masked_last):
        sums = [None] * nh
        for idx, j in enumerate(js):
            for hh in range(nh):
                k, v = _attn_tiles(q_ref, k_ref, v_ref, hh, j)
                s = lax.dot_general(qs[hh], k, (((1,), (1,)), ((), ())), preferred_element_type=F32)
                if masked_last and idx == len(js) - 1:
                    s = jnp.where(causal, s, -jnp.inf)
                pv = jnp.dot(jnp.exp2(s).astype(BF16), v, preferred_element_type=F32)
                sums[hh] = pv if sums[hh] is None else sums[hh] + pv
        return sums

    def trips(n, base):
        def body(jj, carry):
            for hh, part in enumerate(tiles([base + n * jj + u for u in range(n)], False)):
                acc_ref[hh] += part
            return carry
        return body

    def finish(js):
        sums = tiles(js, True)
        lane = lax.broadcasted_iota(jnp.int32, (tq, LANES), 1)
        for p in range(nh // 2):
            a0, a1 = acc_ref[2 * p] + sums[2 * p], acc_ref[2 * p + 1] + sums[2 * p + 1]
            o_ref[0, :, LANES * p:LANES * (p + 1)] = jnp.where(
                lane < HEAD_DIM, a0 / a0[:, HEAD_DIM:HEAD_DIM + 1], a1 / a1[:, 0:1]).astype(BF16)

    n = qi - j0
    acc_ref[...] = jnp.zeros_like(acc_ref)
    lax.fori_loop(0, n >> 2, trips(4, j0), 0)
    for left in range(4):
        @pl.when((n & 3) == left)
        def _(left=left):
            finish([qi - left + u for u in range(left + 1)])


def _attention(q, k, v, bounded, edges, heads_per_step):
    bsz, s, _ = q.shape
    assert TQ == TK

    def call(body, name, nh, is_bounded):
        w = nh * LANES
        in_specs = [
            pl.BlockSpec((1, TQ, w), lambda b, g, i: (b, i, g)),
            pl.BlockSpec((1, s, w), lambda b, g, i: (b, 0, g)),
            pl.BlockSpec((1, s, w), lambda b, g, i: (b, 0, g)),
        ]
        scratch = []
        if is_bounded:
            scratch = [pltpu.VMEM((nh, TQ, LANES), F32)]
            if edges is not None:
                in_specs.insert(0, pl.BlockSpec((1,) + edges.shape[1:], lambda b, g, i: (b, 0, 0, 0)))
        return pl.pallas_call(
            body,
            grid=(bsz, HEADS // nh, s // TQ),
            in_specs=in_specs,
            out_specs=pl.BlockSpec((1, TQ, w // 2), lambda b, g, i: (b, i, g)),
            out_shape=jax.ShapeDtypeStruct((bsz, s, PAIRS * LANES), BF16),
            scratch_shapes=scratch,
            compiler_params=_params("arbitrary", "arbitrary", "arbitrary"),
            name=name,
        )

    fast = call(functools.partial(_attn_bounded_kernel, forgetting=edges is not None), "attn_bounded",
                heads_per_step, True)
    online = call(_attn_online_kernel, "attn_online", 2, False)
    if edges is None:
        return lax.cond(bounded, fast, online, q, k, v)
    return lax.cond(bounded, fast, lambda e, *qkv: online(*qkv), edges, q, k, v)


def _post_kernel(o_ref, x_ref, g1_ref, sh_ref, sc_ref, nrm_ref, wout_ref, wrhl_ref, br_ref,
                 x1_ref, h2_ref, comb_ref):
    tm = x_ref.shape[1]
    y = jnp.dot(o_ref[0], wout_ref[...], preferred_element_type=F32)
    x1 = x_ref[0] + g1_ref[0] * y
    x1_ref[0] = x1
    h2 = _modulated_norm(x1, nrm_ref[...], sc_ref[0], sh_ref[0])
    h_hi = h2.astype(BF16)
    h2_ref[0] = h_hi
    h_lo = (h2 - h_hi.astype(F32)).astype(BF16)
    hw = jnp.dot(h_hi, wrhl_ref[...], preferred_element_type=F32)
    lg = (hw[:, :LANES] + hw[:, LANES:]
          + jnp.dot(h_lo, wrhl_ref[:, :LANES], preferred_element_type=F32)) + br_ref[...]
    lane = lax.broadcasted_iota(jnp.int32, (tm, LANES), 1)
    lanef = lane.astype(F32)
    neg = -jnp.inf
    is_g = jnp.logical_and(lane >= N_EXPERTS, lane < N_EXPERTS + N_GROUPS)
    gl = jnp.where(is_g, lg, neg)
    gmax = jnp.max(gl, axis=-1, keepdims=True)
    gsum = jnp.sum(jnp.exp(gl - gmax), axis=-1, keepdims=True)
    g_top = 1.0 / gsum
    gidx = jnp.min(jnp.where(gl == gmax, lanef - N_EXPERTS, 1e9), axis=-1, keepdims=True)
    lane_grp = (lane >> EPG_SHIFT).astype(F32)
    in_grp = jnp.logical_and(lane < N_EXPERTS, lane_grp == gidx)
    el = jnp.where(in_grp, lg, neg)
    e1 = jnp.max(el, axis=-1, keepdims=True)
    i1 = jnp.min(jnp.where(el == e1, lanef, 1e9), axis=-1, keepdims=True)
    el2 = jnp.where(lanef == i1, neg, el)
    e2 = jnp.max(el2, axis=-1, keepdims=True)
    i2 = jnp.min(jnp.where(el2 == e2, lanef, 1e9), axis=-1, keepdims=True)
    t = jnp.exp(e2 - e1)
    w1 = g_top / (1.0 + t)
    w2 = g_top * t / (1.0 + t)
    comb_ref[0] = jnp.where(lanef == i1, w1, jnp.where(lanef == i2, w2, 0.0))


def _post(o, x, mod, layer, nrm, w_out, w_grp, b_grp, w_rt, b_rt):
    bsz, s, d = x.shape
    tm = TM_POST
    unused = LANES - N_EXPERTS - N_GROUPS
    wr = jnp.concatenate([w_rt, w_grp, jnp.zeros((d, unused), F32)], axis=1)
    wrh = wr.astype(BF16)
    wrhl = jnp.concatenate([wrh, (wr - wrh.astype(F32)).astype(BF16)], axis=1)
    br = jnp.concatenate([b_rt, b_grp, jnp.zeros((unused,), F32)]).reshape(1, LANES)
    full = lambda shape: pl.BlockSpec(shape, lambda b, t: (0,) * len(shape))
    vec = lambda which: _mod_spec(layer, which, bsz, lambda b, t: b)
    tile = lambda w: pl.BlockSpec((1, tm, w), lambda b, t: (b, t, 0))
    return pl.pallas_call(
        _post_kernel,
        grid=(bsz, s // tm),
        in_specs=[tile(d), tile(d), vec(MOD_GATE1), vec(MOD_SHIFT2), vec(MOD_SCALE2), full((1, d)), full((d, d)),
                  full((d, 2 * LANES)), full((1, LANES))],
        out_specs=[tile(d), tile(d), tile(LANES)],
        out_shape=[
            jax.ShapeDtypeStruct((bsz, s, d), F32),
            jax.ShapeDtypeStruct((bsz, s, d), BF16),
            jax.ShapeDtypeStruct((bsz, s, LANES), F32),
        ],
        compiler_params=_params("arbitrary", "arbitrary"),
        name="post_router",
    )(o, x, mod, mod, mod, nrm.reshape(1, d), w_out.astype(BF16), wrhl, br)


def _moe_kernel(h_ref, comb_ref, x1_ref, g2_ref, wg_ref, wu_ref, wd_ref, out_ref):
    tm = h_ref.shape[0]
    h = h_ref[...]
    comb = comb_ref[...]
    lane = lax.broadcasted_iota(jnp.int32, (tm, LANES), 1)
    total = None
    for e in range(N_EXPERTS):
        a = jnp.dot(h, wg_ref[0, e], preferred_element_type=F32)
        u = jnp.dot(h, wu_ref[0, e], preferred_element_type=F32)
        cw = jnp.sum(jnp.where(lane == e, comb, 0.0), axis=-1, keepdims=True)
        hid = (a * _sigmoid(a)) * u * cw
        part = jnp.dot(hid.astype(BF16), wd_ref[0, e], preferred_element_type=F32)
        total = part if total is None else total + part
    out_ref[...] = x1_ref[...] + g2_ref[0] * total


def _moe(h2, comb, x1, mod, w_gate, w_up, w_down, layer):
    bsz, s, d = x1.shape
    t = bsz * s
    tm = TM_MOE
    per_batch = s // tm
    resident = lambda shape: pl.BlockSpec((1, N_EXPERTS) + shape, lambda i: (layer, 0, 0, 0),
                                          pipeline_mode=pl.Buffered(1))
    out = pl.pallas_call(
        _moe_kernel,
        grid=(t // tm,),
        in_specs=[
            pl.BlockSpec((tm, d), lambda i: (i, 0)),
            pl.BlockSpec((tm, LANES), lambda i: (i, 0)),
            pl.BlockSpec((tm, d), lambda i: (i, 0)),
            _mod_spec(layer, MOD_GATE2, bsz, lambda i: i // per_batch),
            resident((d, D_EXPERT)), resident((d, D_EXPERT)), resident((D_EXPERT, d)),
        ],
        out_specs=pl.BlockSpec((tm, d), lambda i: (i, 0)),
        out_shape=jax.ShapeDtypeStruct((t, d), F32),
        compiler_params=_params("arbitrary"),
        name="moe_dense",
    )(h2.reshape(t, d), comb.reshape(t, LANES), x1.reshape(t, d), mod, w_gate, w_up, w_down)
    return out.reshape(bsz, s, d)


def kernel(x, c, positions, w_ada, b_ada, mix_norm, ffn_norm, fox_w_in, fox_b_f, fox_q_norm, fox_k_norm,
           fox_w_out, mla_w_in, mla_q_lat_norm, mla_kv_lat_norm, mla_w_uq, mla_w_ukv, mla_q_norm, mla_k_norm,
           mla_w_out, moe_w_grp, moe_b_grp, moe_w_rt, moe_b_rt, moe_w_gate, moe_w_up, moe_w_down):
    depth = w_ada.shape[0]
    d = x.shape[-1]
    mod = _ada_modulation(c, w_ada, b_ada).reshape(depth * x.shape[0] * 6, 1, d)
    wg, wu, wd = moe_w_gate.astype(BF16), moe_w_up.astype(BF16), moe_w_down.astype(BF16)
    for i in range(depth):
        j = i // 2
        if i % 2 == 0:
            bounded, b_hi, b_lo = _score_bound(fox_q_norm[j], fox_k_norm[j], HEAD_DIM)
            q, k, v, edges = _fox_pre(x, mod, i, mix_norm[i], fox_w_in[j], fox_b_f[j], fox_q_norm[j],
                                      fox_k_norm[j], b_hi, b_lo)
            nh = FOX_HEADS_PER_STEP
            w_out = fox_w_out[j]
        else:
            bounded, b_hi, b_lo = _score_bound(mla_q_norm[j], mla_k_norm[j], MLA_QK)
            q, k, v = _mla_pre(x, positions, mod, i, mix_norm[i], mla_w_in[j], mla_q_lat_norm[j],
                               mla_kv_lat_norm[j], mla_w_uq[j], mla_w_ukv[j], mla_q_norm[j], mla_k_norm[j],
                               b_hi, b_lo)
            nh = MLA_HEADS_PER_STEP
            edges = None
            w_out = mla_w_out[j]
        o = _attention(q, k, v, bounded, edges, nh)
        x1, h2, comb = _post(o, x, mod, i, ffn_norm[i], w_out, moe_w_grp[i], moe_b_grp[i],
                             moe_w_rt[i], moe_b_rt[i])
        x = _moe(h2, comb, x1, mod, wg, wu, wd, i)
    return x
```

```python
import functools
import math

import numpy as np
import jax
import jax.numpy as jnp
from jax import lax
from jax.experimental import pallas as pl
from jax.experimental.pallas import tpu as pltpu

F32 = jnp.float32
BF16 = jnp.bfloat16

D_MODEL = 1024
HEADS = 16
HEAD_DIM = 64
LANES = 128
PAIRS = HEADS // 2
EPS = 1e-6
MLA_Q_RANK = 384
MLA_KV_RANK = 256
MLA_ROPE = 32
MLA_QK = HEAD_DIM + MLA_ROPE
ROPE_THETA = 10000.0
N_GROUPS = 4
EPG = 4
EPG_SHIFT = EPG.bit_length() - 1
N_EXPERTS = N_GROUPS * EPG
assert EPG == 1 << EPG_SHIFT
D_EXPERT = 256

VMEM_LIMIT = 56 * 1024 * 1024
LOG2E = math.log2(math.e)
MAX_FAST_BOUND = 40.0
UNDERFLOW_LOG2 = -136.0
FOX_HEADS_PER_STEP = 2
MLA_HEADS_PER_STEP = 4

TM_PRE = 512
TM_POST = 512
TM_MOE = 1024
TQ = 512
TK = 512
assert TM_PRE == TQ == TK

CUM_HI, CUM_MID, CUM_LO, CUM_ONE = 0, 16, 32, 48


def _params(*sem):
    return pltpu.CompilerParams(dimension_semantics=sem, vmem_limit_bytes=VMEM_LIMIT)


def _sigmoid(x):
    return 1.0 / (1.0 + jnp.exp(-x))


MOD_SHIFT1, MOD_SCALE1, MOD_GATE1, MOD_SHIFT2, MOD_SCALE2, MOD_GATE2 = range(6)


def _mod_spec(layer, which, bsz, batch_of):
    return pl.BlockSpec((1, 1, D_MODEL), lambda *g: ((layer * bsz + batch_of(*g)) * 6 + which, 0, 0))


def _modulated_norm(x, gain, scale, shift):
    ms = jnp.mean(x * x, axis=-1, keepdims=True)
    return (x * lax.rsqrt(ms + EPS) * gain) * (1.0 + scale) + shift


def _ada_kernel(c_ref, w_ref, b_ref, o_ref):
    w = w_ref[0]
    for b in range(c_ref.shape[0]):
        c = c_ref[b]
        act = c * _sigmoid(c)
        o_ref[0, b] = jnp.sum(act * w, axis=0, keepdims=True) + b_ref[0]


def _ada_modulation(c, w_ada, b_ada):
    depth, d, n = w_ada.shape
    bsz = c.shape[0]
    tn = 1536
    out = pl.pallas_call(
        _ada_kernel,
        grid=(depth, n // tn),
        in_specs=[
            pl.BlockSpec((bsz, d, 1), lambda i, j: (0, 0, 0)),
            pl.BlockSpec((1, d, tn), lambda i, j: (i, 0, j)),
            pl.BlockSpec((1, 1, tn), lambda i, j: (i, 0, j)),
        ],
        out_specs=pl.BlockSpec((1, bsz, 1, tn), lambda i, j: (i, 0, 0, j)),
        out_shape=jax.ShapeDtypeStruct((depth, bsz, 1, n), F32),
        compiler_params=_params("arbitrary", "arbitrary"),
        name="ada_mod",
    )(c.reshape(bsz, d, 1), w_ada, b_ada.reshape(depth, 1, n))
    return out.reshape(depth, bsz, n)


def _pair_rmsnorm(xp, gain, lo_half):
    sq = xp * xp
    s_all = jnp.sum(sq, axis=-1, keepdims=True)
    s_lo = jnp.sum(jnp.where(lo_half, sq, 0.0), axis=-1, keepdims=True)
    ms = jnp.where(lo_half, s_lo, s_all - s_lo) * (1.0 / HEAD_DIM)
    return xp * lax.rsqrt(ms + EPS) * gain


def _value_blocks(vp, lo_half, lane):
    even = jnp.where(lo_half, vp, jnp.where(lane == HEAD_DIM, 1.0, 0.0))
    odd = jnp.where(lo_half, jnp.where(lane == 0, 1.0, 0.0), vp)
    return even.astype(BF16), odd.astype(BF16)


def _fox_pre_kernel(x_ref, sh_ref, sc_ref, nrm_ref, wqkv_ref, wf_ref, bf_ref, gq_ref, gk_ref,
                    selq_ref, selk_ref, off_ref, q_ref, k_ref, v_ref, edge_ref, carry_ref):
    tm = x_ref.shape[1]

    @pl.when(pl.program_id(1) == 0)
    def _():
        carry_ref[...] = jnp.zeros_like(carry_ref)

    h = _modulated_norm(x_ref[0], nrm_ref[...], sc_ref[0], sh_ref[0])
    hb = h.astype(BF16)
    proj = jnp.dot(hb, wqkv_ref[...], preferred_element_type=F32)
    fl = jnp.dot(hb, wf_ref[...], preferred_element_type=F32) + bf_ref[...]
    logf = jnp.minimum(fl, 0.0) - jnp.log(1.0 + jnp.exp(-jnp.abs(fl)))

    row = lax.broadcasted_iota(jnp.int32, (tm, tm), 0)
    col = lax.broadcasted_iota(jnp.int32, (tm, tm), 1)
    tri = jnp.where(row >= col, 1.0, 0.0).astype(BF16)
    l_hi = logf.astype(BF16)
    l_r = logf - l_hi.astype(F32)
    l_mid = l_r.astype(BF16)
    l_lo = (l_r - l_mid.astype(F32)).astype(BF16)
    cum = (jnp.dot(tri, l_hi, preferred_element_type=F32) + jnp.dot(tri, l_mid, preferred_element_type=F32)
           + jnp.dot(tri, l_lo, preferred_element_type=F32)) + carry_ref[...]
    carry_ref[...] = cum[tm - 1:tm, :]

    lane = lax.broadcasted_iota(jnp.int32, (tm, LANES), 1)
    cum2 = cum * LOG2E
    edge_ref[0, 0, 0:1, :] = cum2[0:1, :]
    edge_ref[0, 0, 1:2, :] = cum2[tm - 1:tm, :]
    edge_ref[0, 0, 2:8, :] = jnp.zeros((6, LANES), F32)
    hi = cum2.astype(BF16).astype(F32)
    r1 = cum2 - hi
    mid = r1.astype(BF16).astype(F32)
    lo = r1 - mid
    parts = jnp.where(lane < CUM_MID, hi,
                      jnp.where(lane < CUM_LO, mid,
                                jnp.where(lane < CUM_ONE, lo, off_ref[...])))
    pb = parts.astype(BF16)
    augq = jnp.dot(pb, selq_ref[...], preferred_element_type=F32)
    augk = jnp.dot(pb, selk_ref[...], preferred_element_type=F32)

    lo_half = lane < HEAD_DIM
    d = D_MODEL
    for p in range(PAIRS):
        qn = _pair_rmsnorm(proj[:, LANES * p:LANES * (p + 1)], gq_ref[...], lo_half) * (HEAD_DIM ** -0.5 * LOG2E)
        kn = _pair_rmsnorm(proj[:, d + LANES * p:d + LANES * (p + 1)], gk_ref[...], lo_half)
        e0, e1, e2 = 2 * LANES * p, 2 * LANES * p + LANES, 2 * LANES * (p + 1)
        q_ref[0, :, e0:e1] = jnp.where(lo_half, qn, augq[:, e0:e1]).astype(BF16)
        q_ref[0, :, e1:e2] = jnp.where(lo_half, augq[:, e1:e2], qn).astype(BF16)
        k_ref[0, :, e0:e1] = jnp.where(lo_half, kn, augk[:, e0:e1]).astype(BF16)
        k_ref[0, :, e1:e2] = jnp.where(lo_half, augk[:, e1:e2], kn).astype(BF16)
        v_even, v_odd = _value_blocks(proj[:, 2 * d + LANES * p:2 * d + LANES * (p + 1)], lo_half, lane)
        v_ref[0, :, e0:e1] = v_even
        v_ref[0, :, e1:e2] = v_odd


def _fox_selectors():
    selq = np.zeros((LANES, HEADS * LANES), np.float32)
    selk = np.zeros((LANES, HEADS * LANES), np.float32)
    for h in range(HEADS):
        base = h * LANES + (HEAD_DIM if h % 2 == 0 else 0)
        for j, src in enumerate((CUM_HI, CUM_MID, CUM_LO)):
            selq[src + h, base + j] = 1.0
            selq[CUM_ONE, base + 3 + j] = 1.0
            selk[CUM_ONE, base + j] = 1.0
            selk[src + h, base + 3 + j] = -1.0
        for j in range(2):
            selq[CUM_ONE + 1 + j, base + 6 + j] = 1.0
            selk[CUM_ONE, base + 6 + j] = 1.0
    return jnp.asarray(selq, BF16), jnp.asarray(selk, BF16)


def _score_bound(q_gain, k_gain, dim):
    bound = (jnp.max(jnp.abs(q_gain)) * jnp.max(jnp.abs(k_gain)) * (LOG2E * math.sqrt(dim))).astype(F32)
    usable = bound <= MAX_FAST_BOUND
    bound = jnp.where(usable, bound, 0.0)
    hi = bound.astype(BF16).astype(F32)
    return usable, hi, bound - hi


def _fox_pre(x, mod, layer, nrm, w_in, b_f, q_norm, k_norm, bound_hi, bound_lo):
    bsz, s, d = x.shape
    vec = lambda which: _mod_spec(layer, which, bsz, lambda b, t: b)
    tm = TM_PRE
    wqkv = w_in.astype(BF16)
    wf = w_in[:, 3 * d:]
    zpad = jnp.zeros((d, LANES - 3 * HEADS), F32)
    wf3 = jnp.concatenate([wf, wf, wf, zpad], axis=1).astype(BF16)
    bf3 = jnp.concatenate([b_f, b_f, b_f, jnp.zeros((LANES - 3 * HEADS,), F32)]).reshape(1, LANES)
    gq = jnp.tile(q_norm, 2).reshape(1, LANES)
    gk = jnp.tile(k_norm, 2).reshape(1, LANES)
    selq, selk = _fox_selectors()
    off = jnp.concatenate([jnp.zeros((CUM_ONE,), F32), jnp.ones((1,), F32), -bound_hi[None], -bound_lo[None],
                           jnp.zeros((LANES - CUM_ONE - 3,), F32)]).reshape(1, LANES)
    full = lambda shape: pl.BlockSpec(shape, lambda b, t: (0,) * len(shape))
    hq = HEADS * LANES
    head_blocks = pl.BlockSpec((1, tm, hq), lambda b, t: (b, t, 0))
    return pl.pallas_call(
        _fox_pre_kernel,
        grid=(bsz, s // tm),
        in_specs=[
            pl.BlockSpec((1, tm, d), lambda b, t: (b, t, 0)),
            vec(MOD_SHIFT1), vec(MOD_SCALE1), full((1, d)),
            full((d, 3 * d)), full((d, LANES)), full((1, LANES)), full((1, LANES)), full((1, LANES)),
            full((LANES, hq)), full((LANES, hq)), full((1, LANES)),
        ],
        out_specs=[head_blocks] * 3 + [pl.BlockSpec((1, 1, 8, LANES), lambda b, t: (b, t, 0, 0))],
        out_shape=[jax.ShapeDtypeStruct((bsz, s, hq), BF16)] * 3
        + [jax.ShapeDtypeStruct((bsz, s // tm, 8, LANES), F32)],
        scratch_shapes=[pltpu.VMEM((1, LANES), F32)],
        compiler_params=_params("arbitrary", "arbitrary"),
        name="fox_pre",
    )(x, mod, mod, nrm.reshape(1, d), wqkv, wf3, bf3, gq, gk, selq, selk, off)


def _mla_pre_kernel(x_ref, pos_ref, sh_ref, sc_ref, nrm_ref, win_ref, gql_ref, gkvl_ref,
                    wuq_ref, wuqs_ref, wkn_ref, wv_ref, gq_ref, gqs_ref, gk_ref, invf_ref, qoff_ref, koff_ref,
                    q_ref, k_ref, v_ref):
    tm = x_ref.shape[1]
    h = _modulated_norm(x_ref[0], nrm_ref[...], sc_ref[0], sh_ref[0])
    proj = jnp.dot(h.astype(BF16), win_ref[...], preferred_element_type=F32)
    cq = proj[:, :MLA_Q_RANK]
    ckv = proj[:, MLA_Q_RANK:MLA_Q_RANK + MLA_KV_RANK]
    kr = proj[:, MLA_Q_RANK + MLA_KV_RANK:]
    cqn = cq * lax.rsqrt(jnp.mean(cq * cq, axis=-1, keepdims=True) + EPS) * gql_ref[...]
    ckvn = ckv * lax.rsqrt(jnp.mean(ckv * ckv, axis=-1, keepdims=True) + EPS) * gkvl_ref[...]
    cqb = cqn.astype(BF16)
    ckvb = ckvn.astype(BF16)
    qall = jnp.dot(cqb, wuq_ref[...], preferred_element_type=F32)
    qswap = jnp.dot(cqb, wuqs_ref[...], preferred_element_type=F32)
    knall = jnp.dot(ckvb, wkn_ref[...], preferred_element_type=F32)
    vall = jnp.dot(ckvb, wv_ref[...], preferred_element_type=F32)

    lane = lax.broadcasted_iota(jnp.int32, (tm, LANES), 1)
    lo_half = lane < HEAD_DIM
    for p in range(PAIRS):
        v_even, v_odd = _value_blocks(vall[:, LANES * p:LANES * (p + 1)], lo_half, lane)
        v_ref[0, :, 2 * LANES * p:2 * LANES * p + LANES] = v_even
        v_ref[0, :, 2 * LANES * p + LANES:2 * LANES * (p + 1)] = v_odd
    first_half = lane < HEAD_DIM + MLA_ROPE // 2
    ang = pos_ref[0] * invf_ref[...]
    cosf = jnp.cos(ang)
    sinf = jnp.sin(ang)
    sins = jnp.where(first_half, -sinf, sinf)
    half = MLA_ROPE // 2

    def swap_halves(t):
        return jnp.where(first_half, pltpu.roll(t, LANES - half, axis=1), pltpu.roll(t, half, axis=1))

    gk = gk_ref[...]
    q_cos = gq_ref[...] * cosf
    q_sin = gqs_ref[...] * sins
    k_cos = gk * cosf
    sw_k = swap_halves(kr * gk) * sins
    inv_dim = 1.0 / MLA_QK
    for hd in range(HEADS):
        sl = slice(LANES * hd, LANES * (hd + 1))
        qh = qall[:, sl]
        rq = lax.rsqrt(jnp.sum(qh * qh, axis=-1, keepdims=True) * inv_dim + EPS)
        qf = (qh * q_cos + qswap[:, sl] * q_sin) * rq
        q_ref[0, :, sl] = (qf + qoff_ref[...]).astype(BF16)
        kh = knall[:, sl] + kr
        rk = lax.rsqrt(jnp.sum(kh * kh, axis=-1, keepdims=True) * inv_dim + EPS)
        kf = rk * (kh * k_cos + sw_k)
        k_ref[0, :, sl] = (kf + koff_ref[...]).astype(BF16)


def _mla_pre(x, positions, mod, layer, nrm, w_in, q_lat_norm, kv_lat_norm, w_uq, w_ukv, q_norm, k_norm,
             bound_hi, bound_lo):
    bsz, s, d = x.shape
    tm = TM_PRE
    hq = HEADS * LANES
    lat = MLA_Q_RANK + MLA_KV_RANK
    half = MLA_ROPE // 2
    r0, r1, r2 = HEAD_DIM, HEAD_DIM + half, HEAD_DIM + MLA_ROPE

    def widen(t, left):
        return jnp.pad(t, [(0, 0)] * (t.ndim - 1) + [(left, LANES - left - t.shape[-1])])

    def swap_rotary(t):
        return widen(jnp.concatenate([t[..., r1:r2], t[..., r0:r1]], axis=-1), r0)

    win = jnp.concatenate([w_in[:, :lat], widen(w_in[:, lat:], r0)], axis=1).astype(BF16)
    uq = w_uq.reshape(MLA_Q_RANK, HEADS, MLA_QK)
    wuq = widen(uq, 0).reshape(MLA_Q_RANK, hq).astype(BF16)
    wuqs = swap_rotary(uq).reshape(MLA_Q_RANK, hq).astype(BF16)
    ukv = w_ukv.reshape(MLA_KV_RANK, HEADS, 2 * HEAD_DIM)
    wkn = widen(ukv[:, :, :HEAD_DIM], 0).reshape(MLA_KV_RANK, hq).astype(BF16)
    wv = ukv[:, :, HEAD_DIM:].reshape(MLA_KV_RANK, HEADS * HEAD_DIM).astype(BF16)
    gq = widen(q_norm, 0).reshape(1, LANES) * (MLA_QK ** -0.5 * LOG2E)
    gqs = swap_rotary(gq)
    gk = widen(k_norm, 0).reshape(1, LANES)
    inv_freq = ROPE_THETA ** (-jnp.arange(0, half, dtype=F32) / half)
    invf = widen(jnp.concatenate([inv_freq, inv_freq]), r0).reshape(1, LANES)
    pos = positions.astype(F32).reshape(bsz, s, 1)
    qoff = widen(jnp.stack([-bound_hi, -bound_lo]), MLA_QK).reshape(1, LANES)
    koff = jnp.asarray(np.pad(np.ones((1, 2), np.float32), ((0, 0), (MLA_QK, LANES - MLA_QK - 2))))
    full = lambda shape: pl.BlockSpec(shape, lambda b, t: (0,) * len(shape))
    vec = lambda which: _mod_spec(layer, which, bsz, lambda b, t: b)
    head_blocks = pl.BlockSpec((1, tm, hq), lambda b, t: (b, t, 0))
    return pl.pallas_call(
        _mla_pre_kernel,
        grid=(bsz, s // tm),
        in_specs=[
            pl.BlockSpec((1, tm, d), lambda b, t: (b, t, 0)),
            pl.BlockSpec((1, tm, 1), lambda b, t: (b, t, 0)),
            vec(MOD_SHIFT1), vec(MOD_SCALE1), full((1, d)),
            full((d, lat + LANES)), full((1, MLA_Q_RANK)), full((1, MLA_KV_RANK)),
            full((MLA_Q_RANK, hq)), full((MLA_Q_RANK, hq)), full((MLA_KV_RANK, hq)), full((MLA_KV_RANK, d)),
            full((1, LANES)), full((1, LANES)), full((1, LANES)), full((1, LANES)), full((1, LANES)),
            full((1, LANES)),
        ],
        out_specs=[head_blocks] * 3,
        out_shape=[jax.ShapeDtypeStruct((bsz, s, hq), BF16)] * 3,
        compiler_params=_params("arbitrary", "arbitrary"),
        name="mla_pre",
    )(x, pos, mod, mod, nrm.reshape(1, d), win, q_lat_norm.reshape(1, -1), kv_lat_norm.reshape(1, -1),
      wuq, wuqs, wkn, wv, gq, gqs, gk, invf, qoff, koff)


def _attn_tiles(q_ref, k_ref, v_ref, hh, j):
    hs = slice(LANES * hh, LANES * (hh + 1))
    start = pl.multiple_of(j * TK, TK)
    return k_ref[0, pl.ds(start, TK), hs], v_ref[0, pl.ds(start, TK), hs]


def _causal_mask(tq):
    row = lax.broadcasted_iota(jnp.int32, (tq, TK), 0)
    col = lax.broadcasted_iota(jnp.int32, (tq, TK), 1)
    return row >= col


def _attn_online_kernel(q_ref, k_ref, v_ref, o_ref):
    tq = q_ref.shape[1]
    qi = pl.program_id(2)
    causal = _causal_mask(tq)
    qs = [q_ref[0, :, LANES * hh:LANES * (hh + 1)] for hh in range(2)]

    def step(j, carry, masked):
        new = []
        for hh in range(2):
            m, l, acc = carry[hh]
            k, v = _attn_tiles(q_ref, k_ref, v_ref, hh, j)
            s = lax.dot_general(qs[hh], k, (((1,), (1,)), ((), ())), preferred_element_type=F32)
            if masked:
                s = jnp.where(causal, s, -jnp.inf)
            m_new = jnp.maximum(m, jnp.max(s, axis=-1, keepdims=True))
            alpha = jnp.exp2(m - m_new)
            p = jnp.exp2(s - m_new)
            l_new = alpha * l + jnp.sum(p, axis=-1, keepdims=True)
            acc_new = alpha * acc + jnp.dot(p.astype(BF16), v, preferred_element_type=F32)
            new.append((m_new, l_new, acc_new))
        return tuple(new)

    one = (jnp.full((tq, 1), -jnp.inf, F32), jnp.zeros((tq, 1), F32), jnp.zeros((tq, LANES), F32))
    carry = lax.fori_loop(0, qi, functools.partial(step, masked=False), (one, one))
    (_, l0, a0), (_, l1, a1) = step(qi, carry, True)
    lane = lax.broadcasted_iota(jnp.int32, (tq, LANES), 1)
    o_ref[0] = jnp.where(lane < HEAD_DIM, a0 / l0, a1 / l1).astype(BF16)


def _first_key_tile(edge_ref, qi, group, nh):
    first = edge_ref[0, qi, 0:1, :]
    skipped = jnp.zeros((1, LANES), F32)
    for j in range(edge_ref.shape[1]):
        skipped += jnp.where(first - edge_ref[0, j, 1:2, :] <= UNDERFLOW_LOG2, 1.0, 0.0)
    lane = lax.broadcasted_iota(jnp.int32, (1, LANES), 1)
    in_group = jnp.logical_and(lane >= group * nh, lane < (group + 1) * nh)
    return jnp.min(jnp.where(in_group, skipped, float(edge_ref.shape[1]))).astype(jnp.int32)


def _attn_bounded_kernel(*refs, forgetting):
    q_ref, k_ref, v_ref, o_ref, acc_ref = refs[-5:]
    tq = q_ref.shape[1]
    nh = q_ref.shape[2] // LANES
    qi = pl.program_id(2)
    j0 = _first_key_tile(refs[0], qi, pl.program_id(1), nh) if forgetting else 0
    causal = _causal_mask(tq)
    qs = [q_ref[0, :, LANES * hh:LANES * (hh + 1)] for hh in range(nh)]

    def weighted_values(q, k, v, mask):
        s = lax.dot_general(q, k, (((1,), (1,)), ((), ())), preferred_element_type=F32)
        if mask is not None:
            s = jnp.where(mask, s, -jnp.inf)
        return jnp.dot(jnp.exp2(s).astype(BF16), v, preferred_element_type=F32)

    half = tq // 2

    def tiles(js, masked_last):
        sums = [None] * nh
        for idx, j in enumerate(js):
            for hh in range(nh):
                k, v = _attn_tiles(q_ref, k_ref, v_ref, hh, j)
                if masked_last and idx == len(js) - 1:
                    pv = jnp.concatenate([
                        weighted_values(qs[hh][:half], k[:half], v[:half], causal[:half, :half]),
                        weighted_values(qs[hh][half:], k, v, causal[half:, :])], axis=0)
                else:
                    pv = weighted_values(qs[hh], k, v, None)
                sums[hh] = pv if sums[hh] is None else sums[hh] + pv
        return sums

    def trips(n, base):
        def body(jj, carry):
            for hh, part in enumerate(tiles([base + n * jj + u for u in range(n)], False)):
                acc_ref[hh] += part
            return carry
        return body

    def finish(js):
        sums = tiles(js, True)
        lane = lax.broadcasted_iota(jnp.int32, (tq, LANES), 1)
        for p in range(nh // 2):
            a0, a1 = acc_ref[2 * p] + sums[2 * p], acc_ref[2 * p + 1] + sums[2 * p + 1]
            o_ref[0, :, LANES * p:LANES * (p + 1)] = jnp.where(
                lane < HEAD_DIM, a0 / a0[:, HEAD_DIM:HEAD_DIM + 1], a1 / a1[:, 0:1]).astype(BF16)

    n = qi - j0
    acc_ref[...] = jnp.zeros_like(acc_ref)
    lax.fori_loop(0, n >> 2, trips(4, j0), 0)
    for left in range(4):
        @pl.when((n & 3) == left)
        def _(left=left):
            finish([qi - left + u for u in range(left + 1)])


def _attention(q, k, v, bounded, edges, heads_per_step):
    bsz, s, _ = q.shape
    assert TQ == TK

    def call(body, name, nh, is_bounded):
        w = nh * LANES
        in_specs = [
            pl.BlockSpec((1, TQ, w), lambda b, g, i: (b, i, g)),
            pl.BlockSpec((1, s, w), lambda b, g, i: (b, 0, g)),
            pl.BlockSpec((1, s, w), lambda b, g, i: (b, 0, g)),
        ]
        scratch = []
        if is_bounded:
            scratch = [pltpu.VMEM((nh, TQ, LANES), F32)]
            if edges is not None:
                in_specs.insert(0, pl.BlockSpec((1,) + edges.shape[1:], lambda b, g, i: (b, 0, 0, 0)))
        return pl.pallas_call(
            body,
            grid=(bsz, HEADS // nh, s // TQ),
            in_specs=in_specs,
            out_specs=pl.BlockSpec((1, TQ, w // 2), lambda b, g, i: (b, i, g)),
            out_shape=jax.ShapeDtypeStruct((bsz, s, PAIRS * LANES), BF16),
            scratch_shapes=scratch,
            compiler_params=_params("arbitrary", "arbitrary", "arbitrary"),
            name=name,
        )

    fast = call(functools.partial(_attn_bounded_kernel, forgetting=edges is not None), "attn_bounded",
                heads_per_step, True)
    online = call(_attn_online_kernel, "attn_online", 2, False)
    if edges is None:
        return lax.cond(bounded, fast, online, q, k, v)
    return lax.cond(bounded, fast, lambda e, *qkv: online(*qkv), edges, q, k, v)


def _post_kernel(o_ref, x_ref, g1_ref, sh_ref, sc_ref, nrm_ref, wout_ref, wrhl_ref, br_ref,
                 x1_ref, h2_ref, comb_ref):
    tm = x_ref.shape[1]
    y = jnp.dot(o_ref[0], wout_ref[...], preferred_element_type=F32)
    x1 = x_ref[0] + g1_ref[0] * y
    x1_ref[0] = x1
    h2 = _modulated_norm(x1, nrm_ref[...], sc_ref[0], sh_ref[0])
    h_hi = h2.astype(BF16)
    h2_ref[0] = h_hi
    h_lo = (h2 - h_hi.astype(F32)).astype(BF16)
    hw = jnp.dot(h_hi, wrhl_ref[...], preferred_element_type=F32)
    lg = (hw[:, :LANES] + hw[:, LANES:]
          + jnp.dot(h_lo, wrhl_ref[:, :LANES], preferred_element_type=F32)) + br_ref[...]
    lane = lax.broadcasted_iota(jnp.int32, (tm, LANES), 1)
    lanef = lane.astype(F32)
    neg = -jnp.inf
    is_g = jnp.logical_and(lane >= N_EXPERTS, lane < N_EXPERTS + N_GROUPS)
    gl = jnp.where(is_g, lg, neg)
    gmax = jnp.max(gl, axis=-1, keepdims=True)
    gsum = jnp.sum(jnp.exp(gl - gmax), axis=-1, keepdims=True)
    g_top = 1.0 / gsum
    gidx = jnp.min(jnp.where(gl == gmax, lanef - N_EXPERTS, 1e9), axis=-1, keepdims=True)
    lane_grp = (lane >> EPG_SHIFT).astype(F32)
    in_grp = jnp.logical_and(lane < N_EXPERTS, lane_grp == gidx)
    el = jnp.where(in_grp, lg, neg)
    e1 = jnp.max(el, axis=-1, keepdims=True)
    i1 = jnp.min(jnp.where(el == e1, lanef, 1e9), axis=-1, keepdims=True)
    el2 = jnp.where(lanef == i1, neg, el)
    e2 = jnp.max(el2, axis=-1, keepdims=True)
    i2 = jnp.min(jnp.where(el2 == e2, lanef, 1e9), axis=-1, keepdims=True)
    t = jnp.exp(e2 - e1)
    w1 = g_top / (1.0 + t)
    w2 = g_top * t / (1.0 + t)
    comb_ref[0] = jnp.where(lanef == i1, w1, jnp.where(lanef == i2, w2, 0.0))


def _post(o, x, mod, layer, nrm, w_out, w_grp, b_grp, w_rt, b_rt):
    bsz, s, d = x.shape
    tm = TM_POST
    unused = LANES - N_EXPERTS - N_GROUPS
    wr = jnp.concatenate([w_rt, w_grp, jnp.zeros((d, unused), F32)], axis=1)
    wrh = wr.astype(BF16)
    wrhl = jnp.concatenate([wrh, (wr - wrh.astype(F32)).astype(BF16)], axis=1)
    br = jnp.concatenate([b_rt, b_grp, jnp.zeros((unused,), F32)]).reshape(1, LANES)
    full = lambda shape: pl.BlockSpec(shape, lambda b, t: (0,) * len(shape))
    vec = lambda which: _mod_spec(layer, which, bsz, lambda b, t: b)
    tile = lambda w: pl.BlockSpec((1, tm, w), lambda b, t: (b, t, 0))
    return pl.pallas_call(
        _post_kernel,
        grid=(bsz, s // tm),
        in_specs=[tile(d), tile(d), vec(MOD_GATE1), vec(MOD_SHIFT2), vec(MOD_SCALE2), full((1, d)), full((d, d)),
                  full((d, 2 * LANES)), full((1, LANES))],
        out_specs=[tile(d), tile(d), tile(LANES)],
        out_shape=[
            jax.ShapeDtypeStruct((bsz, s, d), F32),
            jax.ShapeDtypeStruct((bsz, s, d), BF16),
            jax.ShapeDtypeStruct((bsz, s, LANES), F32),
        ],
        compiler_params=_params("arbitrary", "arbitrary"),
        name="post_router",
    )(o, x, mod, mod, mod, nrm.reshape(1, d), w_out.astype(BF16), wrhl, br)


def _moe_kernel(h_ref, comb_ref, x1_ref, g2_ref, wg_ref, wu_ref, wd_ref, out_ref):
    tm = h_ref.shape[0]
    h = h_ref[...]
    comb = comb_ref[...]
    lane = lax.broadcasted_iota(jnp.int32, (tm, LANES), 1)
    total = None
    for e in range(N_EXPERTS):
        a = jnp.dot(h, wg_ref[0, e], preferred_element_type=F32)
        u = jnp.dot(h, wu_ref[0, e], preferred_element_type=F32)
        cw = jnp.sum(jnp.where(lane == e, comb, 0.0), axis=-1, keepdims=True)
        hid = (a * _sigmoid(a)) * u * cw
        part = jnp.dot(hid.astype(BF16), wd_ref[0, e], preferred_element_type=F32)
        total = part if total is None else total + part
    out_ref[...] = x1_ref[...] + g2_ref[0] * total


def _moe(h2, comb, x1, mod, w_gate, w_up, w_down, layer):
    bsz, s, d = x1.shape
    t = bsz * s
    tm = TM_MOE
    per_batch = s // tm
    resident = lambda shape: pl.BlockSpec((1, N_EXPERTS) + shape, lambda i: (layer, 0, 0, 0),
                                          pipeline_mode=pl.Buffered(1))
    out = pl.pallas_call(
        _moe_kernel,
        grid=(t // tm,),
        in_specs=[
            pl.BlockSpec((tm, d), lambda i: (i, 0)),
            pl.BlockSpec((tm, LANES), lambda i: (i, 0)),
            pl.BlockSpec((tm, d), lambda i: (i, 0)),
            _mod_spec(layer, MOD_GATE2, bsz, lambda i: i // per_batch),
            resident((d, D_EXPERT)), resident((d, D_EXPERT)), resident((D_EXPERT, d)),
        ],
        out_specs=pl.BlockSpec((tm, d), lambda i: (i, 0)),
        out_shape=jax.ShapeDtypeStruct((t, d), F32),
        compiler_params=_params("arbitrary"),
        name="moe_dense",
    )(h2.reshape(t, d), comb.reshape(t, LANES), x1.reshape(t, d), mod, w_gate, w_up, w_down)
    return out.reshape(bsz, s, d)


def kernel(x, c, positions, w_ada, b_ada, mix_norm, ffn_norm, fox_w_in, fox_b_f, fox_q_norm, fox_k_norm,
           fox_w_out, mla_w_in, mla_q_lat_norm, mla_kv_lat_norm, mla_w_uq, mla_w_ukv, mla_q_norm, mla_k_norm,
           mla_w_out, moe_w_grp, moe_b_grp, moe_w_rt, moe_b_rt, moe_w_gate, moe_w_up, moe_w_down):
    depth = w_ada.shape[0]
    d = x.shape[-1]
    mod = _ada_modulation(c, w_ada, b_ada).reshape(depth * x.shape[0] * 6, 1, d)
    wg, wu, wd = moe_w_gate.astype(BF16), moe_w_up.astype(BF16), moe_w_down.astype(BF16)
    for i in range(depth):
        j = i // 2
        if i % 2 == 0:
            bounded, b_hi, b_lo = _score_bound(fox_q_norm[j], fox_k_norm[j], HEAD_DIM)
            q, k, v, edges = _fox_pre(x, mod, i, mix_norm[i], fox_w_in[j], fox_b_f[j], fox_q_norm[j],
                                      fox_k_norm[j], b_hi, b_lo)
            nh = FOX_HEADS_PER_STEP
            w_out = fox_w_out[j]
        else:
            bounded, b_hi, b_lo = _score_bound(mla_q_norm[j], mla_k_norm[j], MLA_QK)
            q, k, v = _mla_pre(x, positions, mod, i, mix_norm[i], mla_w_in[j], mla_q_lat_norm[j],
                               mla_kv_lat_norm[j], mla_w_uq[j], mla_w_ukv[j], mla_q_norm[j], mla_k_norm[j],
                               b_hi, b_lo)
            nh = MLA_HEADS_PER_STEP
            edges = None
            w_out = mla_w_out[j]
        o = _attention(q, k, v, bounded, edges, nh)
        x1, h2, comb = _post(o, x, mod, i, ffn_norm[i], w_out, moe_w_grp[i], moe_b_grp[i],
                             moe_w_rt[i], moe_b_rt[i])
        x = _moe(h2, comb, x1, mod, wg, wu, wd, i)
    return x
```
